```python
import jax, jax.numpy as jnp
from jax import lax
import numpy as np

D_MODEL = 1024
BATCH = 1
SEQ = 16384
DEPTH = 2
DEC_BATCH = 4
DEC_SEQ = 4096
PAST_LEN = 128

HEAD_DIM = 64
GRID_W = 64
BLOCK = 128
D_FF = 2816
EPS = 1e-6
NEG = -1e30
A_HEADS = 8
A_BRANCHES = ((128, 1), (512, 4), (2048, 16))
B_Q_HEADS = 8
B_KV_HEADS = 2
ROPE_THETA = 10000.0
C_Q_HEADS = 16
C_KV_HEADS = 4
C_WINDOW = 128

N_EVEN = (DEPTH + 1) // 2
N_ODD = DEPTH // 2
A_W = A_HEADS * HEAD_DIM
BQ_W = B_Q_HEADS * HEAD_DIM
BKV_W = B_KV_HEADS * HEAD_DIM
AB_COLS = 3 * A_W + BQ_W + 2 * BKV_W
AB_OUT = A_W + BQ_W
CQ_W = C_Q_HEADS * HEAD_DIM
CKV_W = C_KV_HEADS * HEAD_DIM
C_COLS = CQ_W + 2 * CKV_W

kernel_name = "hybrid_dilated_axial_sink_encoder"


def alibi_slopes(n):
    return jnp.asarray(2.0 ** (-8.0 * (np.arange(n) + 1) / n), dtype=jnp.float32)


def rmsnorm(x, g):
    xf = x.astype(jnp.float32)
    y = xf * lax.rsqrt(jnp.mean(xf * xf, axis=-1, keepdims=True) + EPS)
    return (y * g.astype(jnp.float32)).astype(x.dtype)


def swiglu(x, w1, w3, w2):
    return (jax.nn.silu(x @ w1) * (x @ w3)) @ w2


def dilated_attention(q, k, v, slopes):
    bsz, T, H, Dh = q.shape
    nb = T // BLOCK
    scale = Dh ** -0.5
    offsets = [np.arange(-(w // (2 * d)), w // (2 * d) + 1) * d for (w, d) in A_BRANCHES]
    qb = q.reshape(bsz, nb, BLOCK, H, Dh).swapaxes(0, 1)
    starts = jnp.arange(nb) * BLOCK

    def one_block(args):
        qblk, start = args
        qf = qblk.astype(jnp.float32) * scale
        qpos = start + jnp.arange(BLOCK)
        ms, ls, os_ = [], [], []
        for off in offsets:
            off_j = jnp.asarray(off, dtype=jnp.int32)
            idx = qpos[:, None] + off_j[None, :]
            valid = (idx >= 0) & (idx < T)
            idxc = jnp.clip(idx, 0, T - 1)
            kg = jnp.take(k, idxc, axis=1).astype(jnp.float32)
            vg = jnp.take(v, idxc, axis=1).astype(jnp.float32)
            s = jnp.einsum('bqhd,bqkhd->bhqk', qf, kg)
            s = s - slopes[:, None, None] * jnp.abs(off_j.astype(jnp.float32))[None, None, :]
            s = jnp.where(valid[None, None], s, NEG)
            m = jnp.max(s, axis=-1, keepdims=True)
            p = jnp.exp(s - m)
            l = jnp.sum(p, axis=-1, keepdims=True)
            o = jnp.einsum('bhqk,bqkhd->bhqd', p, vg) / l
            ms.append(m); ls.append(l); os_.append(o)
        m_all = jnp.stack(ms)
        wgt = jnp.exp(m_all - jnp.max(m_all, axis=0, keepdims=True)) * jnp.stack(ls)
        out = jnp.sum(wgt * jnp.stack(os_), axis=0) / jnp.sum(wgt, axis=0)
        return out.transpose(0, 2, 1, 3)

    out = lax.map(one_block, (qb, starts))
    return out.swapaxes(0, 1).reshape(bsz, T, H, Dh)


def axial_rope(T):
    rows = T // GRID_W
    row = jnp.repeat(jnp.arange(rows), GRID_W).astype(jnp.float32)
    col = jnp.tile(jnp.arange(GRID_W), rows).astype(jnp.float32)
    half = HEAD_DIM // 2
    freqs = ROPE_THETA ** (-jnp.arange(0, half, 2, dtype=jnp.float32) / half)
    ang = jnp.concatenate([row[:, None] * freqs, col[:, None] * freqs], axis=-1)
    return jnp.cos(ang)[:, None, :], jnp.sin(ang)[:, None, :]


def apply_rope(x, cos, sin):
    xf = x.astype(jnp.float32)
    x1, x2 = xf[..., 0::2], xf[..., 1::2]
    r = jnp.stack([x1 * cos - x2 * sin, x1 * sin + x2 * cos], axis=-1)
    return r.reshape(xf.shape)


def global_gqa(q, k, v):
    bsz, T, Hq, Dh = q.shape
    Hkv = k.shape[2]
    G = Hq // Hkv
    nb = T // BLOCK
    scale = Dh ** -0.5
    kf = k.astype(jnp.float32)
    vf = v.astype(jnp.float32)
    qb = (q * scale).reshape(bsz, nb, BLOCK, Hkv, G, Dh).swapaxes(0, 1)

    def one_block(qblk):
        s = jnp.einsum('bqhgd,bkhd->bhgqk', qblk, kf)
        p = jax.nn.softmax(s, axis=-1)
        return jnp.einsum('bhgqk,bkhd->bqhgd', p, vf)

    out = lax.map(one_block, qb)
    return out.swapaxes(0, 1).reshape(bsz, T, Hq, Dh)


def banded_sink_gqa(q, k, v, sinks, slopes):
    bsz, T, Hq, Dh = q.shape
    Hkv = k.shape[2]
    G = Hq // Hkv
    nb = T // BLOCK
    scale = Dh ** -0.5
    qb = q.astype(jnp.float32).reshape(bsz, nb, BLOCK, Hkv, G, Dh) * scale

    def band(x):
        xp = jnp.pad(x.astype(jnp.float32), ((0, 0), (BLOCK, BLOCK), (0, 0), (0, 0)))
        xp = xp.reshape(bsz, nb + 2, BLOCK, Hkv, Dh)
        return jnp.concatenate([xp[:, :-2], xp[:, 1:-1], xp[:, 2:]], axis=2)

    kw, vw = band(k), band(v)
    rel = (jnp.arange(3 * BLOCK) - BLOCK)[None, :] - jnp.arange(BLOCK)[:, None]
    kpos = (jnp.arange(nb) * BLOCK)[:, None] + jnp.arange(3 * BLOCK)[None, :] - BLOCK
    valid = (jnp.abs(rel)[None] <= C_WINDOW) & ((kpos >= 0) & (kpos < T))[:, None, :]
    s = jnp.einsum('bnqhgd,bnkhd->bnhgqk', qb, kw)
    s = s - slopes.reshape(Hkv, G)[:, :, None, None] * jnp.abs(rel).astype(jnp.float32)
    s = jnp.where(valid[None, :, None, None], s, NEG)
    sk = sinks.astype(jnp.float32).reshape(Hkv, G)[:, :, None, None]
    m = jnp.maximum(jnp.max(s, axis=-1, keepdims=True), sk)
    p = jnp.exp(s - m)
    denom = jnp.sum(p, axis=-1, keepdims=True) + jnp.exp(sk - m)
    o = jnp.einsum('bnhgqk,bnkhd->bnqhgd', p / denom, vw)
    return o.reshape(bsz, T, Hq, Dh)


def mixer_ab(u, w_in, w_out, q_gain, k_gain):
    bsz, T, _ = u.shape
    proj = u @ w_in
    splits = list(np.cumsum([A_W, A_W, A_W, BQ_W, BKV_W]))
    aq, ak, av, bq, bk, bv = jnp.split(proj, splits, axis=-1)
    shp = lambda t, h: t.reshape(bsz, T, h, HEAD_DIM)
    a_out = dilated_attention(shp(aq, A_HEADS), shp(ak, A_HEADS), shp(av, A_HEADS), alibi_slopes(A_HEADS))
    cos, sin = axial_rope(T)
    bqh = apply_rope(rmsnorm(shp(bq, B_Q_HEADS), q_gain), cos, sin)
    bkh = apply_rope(rmsnorm(shp(bk, B_KV_HEADS), k_gain), cos, sin)
    b_out = global_gqa(bqh, bkh, shp(bv, B_KV_HEADS))
    cat = jnp.concatenate([a_out.reshape(bsz, T, A_W), b_out.reshape(bsz, T, BQ_W)], axis=-1).astype(u.dtype)
    return cat @ w_out


def mixer_c(u, w_in, w_out, sinks):
    bsz, T, _ = u.shape
    proj = u @ w_in
    cq, ck, cv = jnp.split(proj, [CQ_W, CQ_W + CKV_W], axis=-1)
    o = banded_sink_gqa(cq.reshape(bsz, T, C_Q_HEADS, HEAD_DIM), ck.reshape(bsz, T, C_KV_HEADS, HEAD_DIM),
                        cv.reshape(bsz, T, C_KV_HEADS, HEAD_DIM), sinks, alibi_slopes(C_Q_HEADS))
    return o.reshape(bsz, T, CQ_W).astype(u.dtype) @ w_out


def trunk(x, norm_g, ffn_w1, ffn_w3, ffn_w2, ab_w_in, ab_w_out, ab_q_gain, ab_k_gain,
          c_w_in, c_w_out, c_sink, final_g):
    h = x
    for layer in range(DEPTH):
        h = h + 0.5 * swiglu(rmsnorm(h, norm_g[layer, 0]), ffn_w1[layer, 0], ffn_w3[layer, 0], ffn_w2[layer, 0])
        u = rmsnorm(h, norm_g[layer, 1])
        i = layer // 2
        if layer % 2 == 0:
            h = h + mixer_ab(u, ab_w_in[i], ab_w_out[i], ab_q_gain[i], ab_k_gain[i])
        else:
            h = h + mixer_c(u, c_w_in[i], c_w_out[i], c_sink[i])
        h = h + 0.5 * swiglu(rmsnorm(h, norm_g[layer, 2]), ffn_w1[layer, 1], ffn_w3[layer, 1], ffn_w2[layer, 1])
    return rmsnorm(h, final_g)


def setup_inputs(seed: int = 0) -> dict:
    key = jax.random.key(seed)
    ks = jax.random.split(key, 16)
    f32 = jnp.float32
    nrm = lambda k, shape, fan_in: jax.random.normal(k, shape, f32) * (fan_in ** -0.5)
    return {
        "x_prompt": jax.random.normal(ks[0], (BATCH, SEQ, D_MODEL), f32),
        "x_sample": jax.random.normal(ks[1], (DEC_BATCH, DEC_SEQ, D_MODEL), f32),
        "norm_g": 1.0 + 0.01 * jax.random.normal(ks[2], (DEPTH, 3, D_MODEL), f32),
        "ffn_w1": nrm(ks[3], (DEPTH, 2, D_MODEL, D_FF), D_MODEL),
        "ffn_w3": nrm(ks[4], (DEPTH, 2, D_MODEL, D_FF), D_MODEL),
        "ffn_w2": nrm(ks[5], (DEPTH, 2, D_FF, D_MODEL), D_FF),
        "ab_w_in": nrm(ks[6], (N_EVEN, D_MODEL, AB_COLS), D_MODEL),
        "ab_w_out": nrm(ks[7], (N_EVEN, AB_OUT, D_MODEL), AB_OUT),
        "ab_q_gain": 1.0 + 0.01 * jax.random.normal(ks[8], (N_EVEN, HEAD_DIM), f32),
        "ab_k_gain": 1.0 + 0.01 * jax.random.normal(ks[9], (N_EVEN, HEAD_DIM), f32),
        "c_w_in": nrm(ks[10], (N_ODD, D_MODEL, C_COLS), D_MODEL),
        "c_w_out": nrm(ks[11], (N_ODD, CQ_W, D_MODEL), CQ_W),
        "c_sink": jax.random.normal(ks[12], (N_ODD, C_Q_HEADS), f32),
        "final_g": 1.0 + 0.01 * jax.random.normal(ks[13], (D_MODEL,), f32),
    }


def reference(x_prompt, x_sample, norm_g, ffn_w1, ffn_w3, ffn_w2, ab_w_in, ab_w_out, ab_q_gain, ab_k_gain,
              c_w_in, c_w_out, c_sink, final_g):
    y_prompt = trunk(x_prompt, norm_g, ffn_w1, ffn_w3, ffn_w2, ab_w_in, ab_w_out, ab_q_gain, ab_k_gain,
                     c_w_in, c_w_out, c_sink, final_g)
    y_sample = trunk(x_sample, norm_g, ffn_w1, ffn_w3, ffn_w2, ab_w_in, ab_w_out, ab_q_gain, ab_k_gain,
                     c_w_in, c_w_out, c_sink, final_g)
    return (y_prompt, y_sample)
```

```python
import functools

import numpy as np
import jax
import jax.numpy as jnp
from jax import lax
from jax.experimental import pallas as pl
from jax.experimental.pallas import tpu as pltpu

F32 = jnp.float32
BF16 = jnp.bfloat16

D_MODEL = 1024
HEAD_DIM = 64
D_FF = 2816
EPS = 1e-6
NEG = -1e30
GRID_W = 64
ROPE_THETA = 10000.0
A_HEADS = 8
A_BRANCHES = ((128, 1), (512, 4), (2048, 16))
B_Q_HEADS = 8
B_KV_HEADS = 2
C_Q_HEADS = 16
C_KV_HEADS = 4
C_WINDOW = 128
A_W = A_HEADS * HEAD_DIM
BQ_W = B_Q_HEADS * HEAD_DIM
BKV_W = B_KV_HEADS * HEAD_DIM
CQ_W = C_Q_HEADS * HEAD_DIM
CKV_W = C_KV_HEADS * HEAD_DIM
SCALE = HEAD_DIM ** -0.5

LANES = 128
N_CLASS = 4
VMEM_LIMIT = 56 * 1024 * 1024

FFN_TM = 512
FF_CHUNK = 256
PROJ_TM = 256
A_TQ = 128
A_SAME_W = 640
A_OTH_W = 192
A_OTH_OFF = 32
B_TQ = 256
B_TK = 512
C_TQ = 128
C_KW = 3 * C_TQ


def _cparams(sem):
    return pltpu.CompilerParams(dimension_semantics=sem, vmem_limit_bytes=VMEM_LIMIT)


def _rms(x, g):
    return x * lax.rsqrt(jnp.mean(x * x, axis=-1, keepdims=True) + EPS) * g


def _dot(a, b):
    return jnp.dot(a, b, preferred_element_type=F32)


def _dot_nt(a, b):
    return lax.dot_general(a, b, (((1,), (1,)), ((), ())), preferred_element_type=F32)


def _resident(shape):
    nd = len(shape)
    return pl.BlockSpec(shape, lambda *_: (0,) * nd)


def _ffn_body(h_ref, g_ref, w1_ref, w3_ref, w2_ref, fg_ref, o_ref, *, final):
    x = h_ref[...]
    n = _rms(x, g_ref[...]).astype(BF16)
    acc = jnp.zeros(x.shape, F32)
    for c in range(D_FF // FF_CHUNK):
        sl = slice(c * FF_CHUNK, (c + 1) * FF_CHUNK)
        a = _dot(n, w1_ref[:, sl])
        b = _dot(n, w3_ref[:, sl])
        hm = (a * jax.nn.sigmoid(a) * b).astype(BF16)
        acc = acc + _dot(hm, w2_ref[sl, :])
    y = x + 0.5 * acc
    if final:
        y = _rms(y, fg_ref[...])
    o_ref[...] = y


def _ffn(h, g, w1, w3, w2, fg, final):
    n_tok = h.shape[0]
    row = pl.BlockSpec((FFN_TM, D_MODEL), lambda i: (i, 0))
    return pl.pallas_call(
        functools.partial(_ffn_body, final=final),
        out_shape=jax.ShapeDtypeStruct(h.shape, F32),
        grid=(n_tok // FFN_TM,),
        in_specs=[row, _resident((1, D_MODEL)), _resident(w1.shape), _resident(w3.shape),
                  _resident(w2.shape), _resident((1, D_MODEL))],
        out_specs=row,
        compiler_params=_cparams(("parallel",)),
        name="ffn_final" if final else "ffn",
    )(h, g, w1, w3, w2, fg)


def _group_mean_sq(x, gmat):
    ss = x * x
    hi = ss.astype(BF16)
    lo = (ss - hi.astype(F32)).astype(BF16)
    return (_dot(hi, gmat) + _dot(lo, gmat)) * (1.0 / HEAD_DIM)


def _rope(y, cos, sin_signed):
    lane = lax.broadcasted_iota(jnp.int32, (y.shape[0], LANES), 1)
    first_half = (lane & (HEAD_DIM // 2)) == 0
    outs = []
    for j in range(y.shape[1] // LANES):
        yb = y[:, j * LANES:(j + 1) * LANES]
        partner = jnp.where(first_half,
                            pltpu.roll(yb, LANES - HEAD_DIM // 2, 1),
                            pltpu.roll(yb, HEAD_DIM // 2, 1))
        outs.append(yb * cos + partner * sin_signed)
    return outs[0] if len(outs) == 1 else jnp.concatenate(outs, axis=1)


def _ab_in_body(h_ref, g_ref, w_ref, qg_ref, kg_ref, cos_ref, sin_ref, gq_ref, gk_ref,
                aq_ref, ak_ref, av_ref, bq_ref, bk_ref, bv_ref, *, nblk):
    i = pl.program_id(2)
    is_halo = jnp.logical_or(i == 0, i == nblk + 1)

    @pl.when(is_halo)
    def _():
        ak_ref[...] = jnp.zeros(ak_ref.shape, BF16)
        av_ref[...] = jnp.zeros(av_ref.shape, BF16)

    @pl.when(jnp.logical_not(is_halo))
    def _():
        u = _rms(h_ref[...], g_ref[...]).astype(BF16)
        proj = _dot(u, w_ref[...])
        aq_ref[...] = (proj[:, 0:A_W] * SCALE).astype(BF16)
        ak_ref[...] = proj[:, A_W:2 * A_W].astype(BF16)
        av_ref[...] = proj[:, 2 * A_W:3 * A_W].astype(BF16)
        o = 3 * A_W
        cos = cos_ref[...]
        sin = sin_ref[...]
        xq = proj[:, o:o + BQ_W]
        yq = xq * lax.rsqrt(_group_mean_sq(xq, gq_ref[...]) + EPS) * qg_ref[...]
        bq_ref[...] = (_rope(yq, cos, sin) * SCALE).astype(BF16)
        xk = proj[:, o + BQ_W:o + BQ_W + BKV_W]
        yk = xk * lax.rsqrt(_group_mean_sq(xk, gk_ref[...]) + EPS) * kg_ref[...]
        bk_ref[...] = _rope(yk, cos, sin).astype(BF16)
        bv_ref[...] = proj[:, o + BQ_W + BKV_W:o + BQ_W + 2 * BKV_W].astype(BF16)


def _ab_in(h, g, w, qg, kg, cos_t, sin_t, gq, gk):
    bsz, t, _ = h.shape
    tc = t // N_CLASS
    nblk = tc // PROJ_TM
    hv = h.reshape(bsz, tc, N_CLASS * D_MODEL)

    def inner(i):
        return jnp.clip(i - 1, 0, nblk - 1)

    def cm(width):
        return pl.BlockSpec((None, None, PROJ_TM, width), lambda b, c, i: (b, c, inner(i), 0))

    def cm_halo(width):
        return pl.BlockSpec((None, None, PROJ_TM, width), lambda b, c, i: (b, c, i, 0))

    tab = pl.BlockSpec((None, PROJ_TM, LANES), lambda b, c, i: (c, inner(i), 0))
    sds = jax.ShapeDtypeStruct
    return pl.pallas_call(
        functools.partial(_ab_in_body, nblk=nblk),
        out_shape=(sds((bsz, N_CLASS, tc, A_W), BF16),
                   sds((bsz, N_CLASS, tc + 2 * PROJ_TM, A_W), BF16),
                   sds((bsz, N_CLASS, tc + 2 * PROJ_TM, A_W), BF16),
                   sds((bsz, N_CLASS, tc, BQ_W), BF16),
                   sds((bsz, N_CLASS, tc, BKV_W), BF16),
                   sds((bsz, N_CLASS, tc, BKV_W), BF16)),
        grid=(bsz, N_CLASS, nblk + 2),
        in_specs=[pl.BlockSpec((None, PROJ_TM, D_MODEL), lambda b, c, i: (b, inner(i), c)),
                  _resident((1, D_MODEL)), _resident(w.shape), _resident((1, BQ_W)),
                  _resident((1, BKV_W)), tab, tab, _resident(gq.shape), _resident(gk.shape)],
        out_specs=(cm(A_W), cm_halo(A_W), cm_halo(A_W), cm(BQ_W), cm(BKV_W), cm(BKV_W)),
        compiler_params=_cparams(("parallel", "parallel", "arbitrary")),
        name="ab_in_proj",
    )(hv, g, w, qg, kg, cos_t, sin_t, gq, gk)


def _valid_bias(start, width, limit):
    pos = start + lax.broadcasted_iota(jnp.int32, (1, width), 1)
    return jnp.where(jnp.logical_and(pos >= 0, pos < limit), 0.0, NEG).astype(F32)


def _attn_a_body(q_ref, k_ref, v_ref, ts_ref, to_ref, o_ref, *, tc):
    c = pl.program_id(2)
    u0 = pl.program_id(3) * A_TQ
    q = q_ref[...]
    lane = lax.broadcasted_iota(jnp.int32, q.shape, 1)
    lower = lane < HEAD_DIM
    zero = jnp.zeros(q.shape, q.dtype)

    s_start = pl.multiple_of(u0 + (PROJ_TM - 256), A_TQ)
    o_start = pl.multiple_of(u0 + (PROJ_TM - A_OTH_OFF), 32)
    ks = [k_ref[c, pl.ds(s_start, A_SAME_W), :]]
    vs = [v_ref[c, pl.ds(s_start, A_SAME_W), :]]
    variants = []
    for d in range(1, N_CLASS):
        cp = lax.rem(c + d, N_CLASS)
        ks.append(k_ref[cp, pl.ds(o_start, A_OTH_W), :])
        vs.append(v_ref[cp, pl.ds(o_start, A_OTH_W), :])
        variants.append(cp - c + (N_CLASS - 1))
    vb_same = _valid_bias(u0 - 256, A_SAME_W, tc)
    vb_oth = _valid_bias(u0 - A_OTH_OFF, A_OTH_W, tc)

    outs = []
    for hd in range(2):
        qp = jnp.where(lower if hd == 0 else jnp.logical_not(lower), q, zero)
        ss = [_dot_nt(qp, ks[0]) + ts_ref[hd] + vb_same]
        for d in range(1, N_CLASS):
            ss.append(_dot_nt(qp, ks[d]) + to_ref[hd, variants[d - 1]] + vb_oth)
        m = jnp.max(ss[0], axis=-1, keepdims=True)
        for s in ss[1:]:
            m = jnp.maximum(m, jnp.max(s, axis=-1, keepdims=True))
        l = jnp.zeros_like(m)
        acc = jnp.zeros((A_TQ, LANES), F32)
        for s, v in zip(ss, vs):
            p = jnp.exp(s - m)
            l = l + jnp.sum(p, axis=-1, keepdims=True)
            acc = acc + _dot(p.astype(BF16), v)
        outs.append(acc / l)
    o_ref[...] = jnp.where(lower, outs[0], outs[1]).astype(o_ref.dtype)


def _attn_a(aq, ak, av, tab_same, tab_oth):
    bsz, _, tc, _ = aq.shape
    qspec = pl.BlockSpec((None, None, A_TQ, LANES), lambda b, p, c, i: (b, c, i, p))
    kvspec = pl.BlockSpec((None, N_CLASS, tc + 2 * PROJ_TM, LANES), lambda b, p, c, i: (b, 0, 0, p))
    return pl.pallas_call(
        functools.partial(_attn_a_body, tc=tc),
        out_shape=jax.ShapeDtypeStruct(aq.shape, BF16),
        grid=(bsz, A_HEADS // 2, N_CLASS, tc // A_TQ),
        in_specs=[qspec, kvspec, kvspec,
                  pl.BlockSpec((2, A_TQ, A_SAME_W), lambda b, p, c, i: (p, 0, 0)),
                  pl.BlockSpec((2, 2 * N_CLASS - 1, A_TQ, A_OTH_W), lambda b, p, c, i: (p, 0, 0, 0))],
        out_specs=qspec,
        compiler_params=_cparams(("parallel", "parallel", "arbitrary", "arbitrary")),
        name="attn_dilated",
    )(aq, ak, av, tab_same, tab_oth)


def _attn_b_body(q_ref, k_ref, v_ref, o_ref, qp_ref, m_ref, l_ref, acc_ref, *, t):
    lane = lax.broadcasted_iota(jnp.int32, (B_TQ, LANES), 1)
    lower = lane < HEAD_DIM
    zero = jnp.zeros((B_TQ, LANES), BF16)
    n_col = BQ_W // LANES
    for j in range(n_col):
        qb = q_ref[:, j * LANES:(j + 1) * LANES]
        qp_ref[0, j * B_TQ:(j + 1) * B_TQ, :] = jnp.where(lower, qb, zero)
        qp_ref[1, j * B_TQ:(j + 1) * B_TQ, :] = jnp.where(lower, zero, qb)
    m_ref[...] = jnp.full(m_ref.shape, NEG, F32)
    l_ref[...] = jnp.zeros(l_ref.shape, F32)
    acc_ref[...] = jnp.zeros(acc_ref.shape, F32)

    def step(kb, carry):
        start = pl.multiple_of(kb * B_TK, B_TK)
        k = k_ref[pl.ds(start, B_TK), :]
        v = v_ref[pl.ds(start, B_TK), :]
        for g in range(B_KV_HEADS):
            s = _dot_nt(qp_ref[g], k)
            m_old = m_ref[g]
            m_new = jnp.maximum(m_old, jnp.max(s, axis=-1, keepdims=True))
            alpha = jnp.exp(m_old - m_new)
            p = jnp.exp(s - m_new)
            l_ref[g] = alpha * l_ref[g] + jnp.sum(p, axis=-1, keepdims=True)
            acc_ref[g] = alpha * acc_ref[g] + _dot(p.astype(BF16), v)
            m_ref[g] = m_new
        return carry

    lax.fori_loop(0, t // B_TK, step, 0)
    o0 = acc_ref[0] / l_ref[0]
    o1 = acc_ref[1] / l_ref[1]
    for j in range(n_col):
        rs = slice(j * B_TQ, (j + 1) * B_TQ)
        o_ref[:, j * LANES:(j + 1) * LANES] = jnp.where(lower, o0[rs], o1[rs]).astype(o_ref.dtype)


def _attn_b(bq, bk, bv):
    bsz, t, _ = bq.shape
    rows = (B_Q_HEADS // B_KV_HEADS) * B_TQ
    qspec = pl.BlockSpec((None, B_TQ, BQ_W), lambda b, i: (b, i, 0))
    kvspec = pl.BlockSpec((None, t, BKV_W), lambda b, i: (b, 0, 0))
    return pl.pallas_call(
        functools.partial(_attn_b_body, t=t),
        out_shape=jax.ShapeDtypeStruct(bq.shape, BF16),
        grid=(bsz, t // B_TQ),
        in_specs=[qspec, kvspec, kvspec],
        out_specs=qspec,
        scratch_shapes=[pltpu.VMEM((B_KV_HEADS, rows, LANES), BF16),
                        pltpu.VMEM((B_KV_HEADS, rows, 1), F32),
                        pltpu.VMEM((B_KV_HEADS, rows, 1), F32),
                        pltpu.VMEM((B_KV_HEADS, rows, LANES), F32)],
        compiler_params=_cparams(("parallel", "arbitrary")),
        name="attn_global",
    )(bq, bk, bv)


def _ab_out_body(h_ref, a_ref, b_ref, wa_ref, wb_ref, o_ref):
    o_ref[...] = h_ref[...] + _dot(a_ref[...], wa_ref[...]) + _dot(b_ref[...], wb_ref[...])


def _ab_out(h, a_out, b_out, wa, wb):
    bsz, t, _ = h.shape
    tc = t // N_CLASS
    hv = h.reshape(bsz, tc, N_CLASS * D_MODEL)
    hspec = pl.BlockSpec((None, PROJ_TM, D_MODEL), lambda b, c, i: (b, i, c))
    aspec = pl.BlockSpec((None, None, PROJ_TM, A_W), lambda b, c, i: (b, c, i, 0))
    out = pl.pallas_call(
        _ab_out_body,
        out_shape=jax.ShapeDtypeStruct(hv.shape, F32),
        grid=(bsz, N_CLASS, tc // PROJ_TM),
        in_specs=[hspec, aspec, aspec, _resident(wa.shape), _resident(wb.shape)],
        out_specs=hspec,
        compiler_params=_cparams(("parallel", "parallel", "parallel")),
        name="ab_out_proj",
    )(hv, a_out, b_out, wa, wb)
    return out.reshape(h.shape)


def _c_in_body(h_ref, g_ref, w_ref, q_ref, k_ref, v_ref, *, nblk):
    i = pl.program_id(1)
    is_halo = jnp.logical_or(i == 0, i == nblk + 1)

    @pl.when(is_halo)
    def _():
        k_ref[...] = jnp.zeros(k_ref.shape, BF16)
        v_ref[...] = jnp.zeros(v_ref.shape, BF16)

    @pl.when(jnp.logical_not(is_halo))
    def _():
        u = _rms(h_ref[...], g_ref[...]).astype(BF16)
        proj = _dot(u, w_ref[...])
        q_ref[...] = (proj[:, 0:CQ_W] * SCALE).astype(BF16)
        k_ref[...] = proj[:, CQ_W:CQ_W + CKV_W].astype(BF16)
        v_ref[...] = proj[:, CQ_W + CKV_W:CQ_W + 2 * CKV_W].astype(BF16)


def _c_in(h, g, w):
    bsz, t, _ = h.shape
    nblk = t // PROJ_TM

    def inner(i):
        return jnp.clip(i - 1, 0, nblk - 1)

    sds = jax.ShapeDtypeStruct
    return pl.pallas_call(
        functools.partial(_c_in_body, nblk=nblk),
        out_shape=(sds((bsz, t, CQ_W), BF16),
                   sds((bsz, t + 2 * PROJ_TM, CKV_W), BF16),
                   sds((bsz, t + 2 * PROJ_TM, CKV_W), BF16)),
        grid=(bsz, nblk + 2),
        in_specs=[pl.BlockSpec((None, PROJ_TM, D_MODEL), lambda b, i: (b, inner(i), 0)),
                  _resident((1, D_MODEL)), _resident(w.shape)],
        out_specs=(pl.BlockSpec((None, PROJ_TM, CQ_W), lambda b, i: (b, inner(i), 0)),
                   pl.BlockSpec((None, PROJ_TM, CKV_W), lambda b, i: (b, i, 0)),
                   pl.BlockSpec((None, PROJ_TM, CKV_W), lambda b, i: (b, i, 0))),
        compiler_params=_cparams(("parallel", "arbitrary")),
        name="c_in_proj",
    )(h, g, w)


def _attn_c_body(q_ref, k_ref, v_ref, tab_ref, sink_ref, o_ref, *, t):
    t0 = pl.program_id(2) * C_TQ
    n_col = (CQ_W // 2) // LANES
    lane = lax.broadcasted_iota(jnp.int32, (C_TQ, LANES), 1)
    lower = lane < HEAD_DIM
    zero = jnp.zeros((C_TQ, LANES), BF16)
    start = pl.multiple_of(t0 + (PROJ_TM - C_TQ), C_TQ)
    k = k_ref[pl.ds(start, C_KW), :]
    v = v_ref[pl.ds(start, C_KW), :]
    vb = _valid_bias(t0 - C_TQ, C_KW, t)
    qbs = [q_ref[:, j * LANES:(j + 1) * LANES] for j in range(n_col)]
    outs = []
    for half in range(2):
        keep = lower if half == 0 else jnp.logical_not(lower)
        qp = jnp.concatenate([jnp.where(keep, qb, zero) for qb in qbs], axis=0)
        s = _dot_nt(qp, k) + tab_ref[half] + vb
        sink = sink_ref[half][:, 0:1]
        m = jnp.maximum(jnp.max(s, axis=-1, keepdims=True), sink)
        p = jnp.exp(s - m)
        denom = jnp.sum(p, axis=-1, keepdims=True) + jnp.exp(sink - m)
        outs.append(_dot(p.astype(BF16), v) / denom)
    for j in range(n_col):
        rs = slice(j * C_TQ, (j + 1) * C_TQ)
        o_ref[:, j * LANES:(j + 1) * LANES] = jnp.where(lower, outs[0][rs], outs[1][rs]).astype(o_ref.dtype)


def _attn_c(cq, ck, cv, tab, sink_rows):
    bsz, t, _ = cq.shape
    n_pair = C_KV_HEADS // 2
    rows = (CQ_W // 2 // LANES) * C_TQ
    qspec = pl.BlockSpec((None, C_TQ, CQ_W // n_pair), lambda b, p, i: (b, i, p))
    kvspec = pl.BlockSpec((None, t + 2 * PROJ_TM, LANES), lambda b, p, i: (b, 0, p))
    return pl.pallas_call(
        functools.partial(_attn_c_body, t=t),
        out_shape=jax.ShapeDtypeStruct(cq.shape, BF16),
        grid=(bsz, n_pair, t // C_TQ),
        in_specs=[qspec, kvspec, kvspec,
                  pl.BlockSpec((None, 2, rows, C_KW), lambda b, p, i: (p, 0, 0, 0)),
                  pl.BlockSpec((None, 2, rows, LANES), lambda b, p, i: (p, 0, 0, 0))],
        out_specs=qspec,
        compiler_params=_cparams(("parallel", "parallel", "arbitrary")),
        name="attn_banded",
    )(cq, ck, cv, tab, sink_rows)


def _c_out_body(h_ref, a_ref, w_ref, o_ref):
    o_ref[...] = h_ref[...] + _dot(a_ref[...], w_ref[...])


def _c_out(h, a, w):
    n_tok = h.shape[0]
    return pl.pallas_call(
        _c_out_body,
        out_shape=jax.ShapeDtypeStruct(h.shape, F32),
        grid=(n_tok // FFN_TM,),
        in_specs=[pl.BlockSpec((FFN_TM, D_MODEL), lambda i: (i, 0)),
                  pl.BlockSpec((FFN_TM, CQ_W), lambda i: (i, 0)), _resident(w.shape)],
        out_specs=pl.BlockSpec((FFN_TM, D_MODEL), lambda i: (i, 0)),
        compiler_params=_cparams(("parallel",)),
        name="c_out_proj",
    )(h, a, w)


def _alibi_slopes(n):
    return 2.0 ** (-8.0 * (np.arange(n) + 1) / n)


def _dilated_tables():
    slopes = _alibi_slopes(A_HEADS)[:, None, None]

    def table(off):
        mult = np.zeros(off.shape)
        for w, d in A_BRANCHES:
            mult += (off % d == 0) & (np.abs(off) <= w // 2)
        bias = -slopes * np.abs(off)[None] + np.log(np.maximum(mult, 1.0))[None]
        return np.where(mult[None] > 0, bias, NEG).astype(np.float32)

    qi = np.arange(A_TQ)[:, None]
    same = table(N_CLASS * (np.arange(A_SAME_W)[None, :] - 256 - qi))
    oth = []
    for delta in range(-(N_CLASS - 1), N_CLASS):
        off = N_CLASS * (np.arange(A_OTH_W)[None, :] - A_OTH_OFF - qi) + delta
        oth.append(table(off) if delta != 0 else np.full((A_HEADS, A_TQ, A_OTH_W), NEG, np.float32))
    return jnp.asarray(same), jnp.asarray(np.stack(oth, axis=1))


def _c_head_order():
    grp = C_Q_HEADS // C_KV_HEADS
    order = []
    for p in range(C_KV_HEADS // 2):
        for j in range(grp):
            order += [2 * grp * p + j, 2 * grp * p + grp + j]
    return order


def _banded_table():
    slopes = _alibi_slopes(C_Q_HEADS)
    rel = (np.arange(C_KW)[None, :] - C_TQ) - np.arange(C_TQ)[:, None]
    grp = C_Q_HEADS // C_KV_HEADS
    tab = np.zeros((C_KV_HEADS // 2, 2, grp, C_TQ, C_KW), np.float32)
    for p in range(C_KV_HEADS // 2):
        for half in range(2):
            for j in range(grp):
                h = 2 * grp * p + grp * half + j
                tab[p, half, j] = np.where(np.abs(rel) <= C_WINDOW, -slopes[h] * np.abs(rel), NEG)
    return jnp.asarray(tab.reshape(C_KV_HEADS // 2, 2, grp * C_TQ, C_KW))


def _rope_tables(t):
    n_rows = t // GRID_W
    half = HEAD_DIM // 2
    row = jnp.repeat(jnp.arange(n_rows), GRID_W).astype(F32)
    col = jnp.tile(jnp.arange(GRID_W), n_rows).astype(F32)
    fr = ROPE_THETA ** (-jnp.arange(0, half, 2, dtype=F32) / half)
    ang = jnp.concatenate([row[:, None] * fr, col[:, None] * fr], axis=-1)
    cos, sin = jnp.cos(ang), jnp.sin(ang)
    cos_h = jnp.concatenate([cos, cos], axis=-1)
    sin_h = jnp.concatenate([-sin, sin], axis=-1)

    def cm(x):
        x = jnp.concatenate([x, x], axis=-1)
        return x.reshape(t // N_CLASS, N_CLASS, LANES).transpose(1, 0, 2)

    return cm(cos_h), cm(sin_h)


_DEINT = np.concatenate([np.arange(0, HEAD_DIM, 2), np.arange(1, HEAD_DIM, 2)])
_B_HEAD_ORDER = [0, 4, 1, 5, 2, 6, 3, 7]


def _head_cols(heads, base, dims):
    return np.concatenate([base + h * HEAD_DIM + dims for h in heads])


def _prep_ab_weights(w_in, w_out, q_gain, k_gain):
    nat = np.arange(HEAD_DIM)
    o = 3 * A_W
    cols = np.concatenate([
        np.arange(o),
        _head_cols(_B_HEAD_ORDER, o, _DEINT),
        _head_cols(range(B_KV_HEADS), o + BQ_W, _DEINT),
        np.arange(o + BQ_W + BKV_W, o + BQ_W + 2 * BKV_W)])
    w = w_in[:, cols].astype(BF16)
    wa = w_out[:A_W].astype(BF16)
    wb = w_out[_head_cols(_B_HEAD_ORDER, A_W, nat)].astype(BF16)
    qg = jnp.tile(q_gain[_DEINT], B_Q_HEADS)[None, :]
    kg = jnp.tile(k_gain[_DEINT], B_KV_HEADS)[None, :]
    return w, wa, wb, qg, kg


def _prep_c_weights(w_in, w_out, sink):
    nat = np.arange(HEAD_DIM)
    order = _c_head_order()
    qcols = _head_cols(order, 0, nat)
    cols = np.concatenate([qcols, np.arange(CQ_W, CQ_W + 2 * CKV_W)])
    w = w_in[:, cols].astype(BF16)
    wo = w_out[qcols].astype(BF16)
    grp = C_Q_HEADS // C_KV_HEADS
    idx = np.array([[[2 * grp * p + grp * half + j for j in range(grp)] for half in range(2)]
                    for p in range(C_KV_HEADS // 2)])
    srows = jnp.repeat(sink[idx], C_TQ, axis=-1)
    srows = jnp.broadcast_to(srows[..., None], srows.shape + (LANES,)).astype(F32)
    return w, wo, srows


def _block_ones(width):
    g = np.arange(width) // HEAD_DIM
    return jnp.asarray((g[:, None] == g[None, :]).astype(np.float32)).astype(BF16)


def _trunk(x, p):
    bsz, t, _ = x.shape
    flat = lambda a: a.reshape(bsz * t, D_MODEL)
    unflat = lambda a: a.reshape(bsz, t, D_MODEL)
    cos_t, sin_t = _rope_tables(t)

    h = _ffn(flat(x), p["g"][0][0], *p["ffn"][0][0], p["fg"], False)
    aq, ak, av, bq, bk, bv = _ab_in(unflat(h), p["g"][0][1], p["ab_w"], p["qg"], p["kg"],
                                    cos_t, sin_t, p["gq"], p["gk"])
    a_out = _attn_a(aq, ak, av, p["tab_same"], p["tab_oth"])
    tc = t // N_CLASS
    b_out = _attn_b(bq.reshape(bsz, t, BQ_W), bk.reshape(bsz, t, BKV_W), bv.reshape(bsz, t, BKV_W))
    h = _ab_out(unflat(h), a_out, b_out.reshape(bsz, N_CLASS, tc, BQ_W), p["ab_wa"], p["ab_wb"])
    h = _ffn(flat(h), p["g"][0][2], *p["ffn"][0][1], p["fg"], False)

    h = _ffn(h, p["g"][1][0], *p["ffn"][1][0], p["fg"], False)
    cq, ck, cv = _c_in(unflat(h), p["g"][1][1], p["c_w"])
    c_out = _attn_c(cq, ck, cv, p["tab_c"], p["sink"])
    h = _c_out(h, c_out.reshape(bsz * t, CQ_W), p["c_wo"])
    h = _ffn(h, p["g"][1][2], *p["ffn"][1][1], p["fg"], True)
    return unflat(h)


def kernel(x_prompt, x_sample, norm_g, ffn_w1, ffn_w3, ffn_w2, ab_w_in, ab_w_out, ab_q_gain, ab_k_gain,
           c_w_in, c_w_out, c_sink, final_g):
    tab_same, tab_oth = _dilated_tables()
    ab_w, ab_wa, ab_wb, qg, kg = _prep_ab_weights(ab_w_in[0], ab_w_out[0], ab_q_gain[0], ab_k_gain[0])
    c_w, c_wo, sink = _prep_c_weights(c_w_in[0], c_w_out[0], c_sink[0])
    p = {
        "g": [[norm_g[l, k][None, :] for k in range(3)] for l in range(2)],
        "ffn": [[(ffn_w1[l, k].astype(BF16), ffn_w3[l, k].astype(BF16), ffn_w2[l, k].astype(BF16))
                 for k in range(2)] for l in range(2)],
        "fg": final_g[None, :],
        "ab_w": ab_w, "ab_wa": ab_wa, "ab_wb": ab_wb, "qg": qg, "kg": kg,
        "gq": _block_ones(BQ_W), "gk": _block_ones(BKV_W),
        "tab_same": tab_same, "tab_oth": tab_oth,
        "c_w": c_w, "c_wo": c_wo, "sink": sink, "tab_c": _banded_table(),
    }
    return _trunk(x_prompt, p), _trunk(x_sample, p)
```

```python
import functools

import numpy as np
import jax
import jax.numpy as jnp
from jax import lax
from jax.experimental import pallas as pl
from jax.experimental.pallas import tpu as pltpu

F32 = jnp.float32
BF16 = jnp.bfloat16

D_MODEL = 1024
HEAD_DIM = 64
D_FF = 2816
EPS = 1e-6
NEG = -1e30
GRID_W = 64
ROPE_THETA = 10000.0
A_HEADS = 8
A_BRANCHES = ((128, 1), (512, 4), (2048, 16))
B_Q_HEADS = 8
B_KV_HEADS = 2
C_Q_HEADS = 16
C_KV_HEADS = 4
C_WINDOW = 128
A_W = A_HEADS * HEAD_DIM
BQ_W = B_Q_HEADS * HEAD_DIM
BKV_W = B_KV_HEADS * HEAD_DIM
CQ_W = C_Q_HEADS * HEAD_DIM
CKV_W = C_KV_HEADS * HEAD_DIM
LOG2E = 1.4426950408889634
QSCALE = HEAD_DIM ** -0.5 * LOG2E

LANES = 128
PAIR_W = 2 * LANES
N_CLASS = 4
VMEM_LIMIT = 56 * 1024 * 1024

FFN_TM = 512
FF_CHUNK = 256
PROJ_TM = 256
A_TQ = 128
A_SAME_W = 640
A_OTH_W = 192
A_OTH_OFF = 32
B_TQ = 256
B_TK = 1024
B_LAG_SOFTMAX = 3
B_LAG_PV = 6
C_TQ = 128
C_KW = 3 * C_TQ


def _cparams(sem):
    return pltpu.CompilerParams(dimension_semantics=sem, vmem_limit_bytes=VMEM_LIMIT)


def _rms(x, g):
    return x * lax.rsqrt(jnp.mean(x * x, axis=-1, keepdims=True) + EPS) * g


def _dot(a, b):
    return jnp.dot(a, b, preferred_element_type=F32)


def _dot_nt(a, b):
    return lax.dot_general(a, b, (((1,), (1,)), ((), ())), preferred_element_type=F32)


def _dot_tn(a, b):
    return lax.dot_general(a, b, (((0,), (0,)), ((), ())), preferred_element_type=F32)


def _resident(shape):
    nd = len(shape)
    return pl.BlockSpec(shape, lambda *_: (0,) * nd)


def _lane_iota(rows):
    return lax.broadcasted_iota(jnp.int32, (rows, LANES), 1)


def _ffn_body(h_ref, g_ref, w1_ref, w3_ref, w2_ref, fg_ref, o_ref, *scratch, final, cm_in, cm_out):
    sub = FFN_TM // N_CLASS
    n_col = D_MODEL // LANES
    if cm_in:
        perm_ref, = scratch
        for c in range(N_CLASS):
            for j in range(n_col):
                perm_ref[j, pl.ds(c, sub, stride=N_CLASS), :] = h_ref[c, :, j * LANES:(j + 1) * LANES]
        x = jnp.concatenate([perm_ref[j] for j in range(n_col)], axis=1)
    else:
        x = h_ref[...]
    n = _rms(x, g_ref[...]).astype(BF16)
    acc = jnp.zeros(x.shape, F32)
    for c in range(D_FF // FF_CHUNK):
        sl = slice(c * FF_CHUNK, (c + 1) * FF_CHUNK)
        a = _dot(n, w1_ref[:, sl])
        b = _dot(n, w3_ref[:, sl])
        hm = (a * jax.nn.sigmoid(a) * b).astype(BF16)
        acc = acc + _dot(hm, w2_ref[sl, :])
    y = x + 0.5 * acc
    if final:
        y = _rms(y, fg_ref[...])
    if cm_out:
        perm_ref, = scratch
        for j in range(n_col):
            perm_ref[j] = y[:, j * LANES:(j + 1) * LANES]
        for c in range(N_CLASS):
            for j in range(n_col):
                o_ref[c, :, j * LANES:(j + 1) * LANES] = perm_ref[j, pl.ds(c, sub, stride=N_CLASS), :]
    else:
        o_ref[...] = y


def _ffn(h, g, w1, w3, w2, fg, *, final=False, cm_in=False, cm_out=False):
    bsz = h.shape[0]
    t = h.shape[1] * h.shape[2] if cm_in else h.shape[1]
    tc = t // N_CLASS
    sub = FFN_TM // N_CLASS
    nat = pl.BlockSpec((None, FFN_TM, D_MODEL), lambda b, i: (b, i, 0))
    cm = pl.BlockSpec((None, N_CLASS, sub, D_MODEL), lambda b, i: (b, 0, i, 0))
    out_shape = (bsz, N_CLASS, tc, D_MODEL) if cm_out else (bsz, t, D_MODEL)
    return pl.pallas_call(
        functools.partial(_ffn_body, final=final, cm_in=cm_in, cm_out=cm_out),
        out_shape=jax.ShapeDtypeStruct(out_shape, F32),
        grid=(bsz, t // FFN_TM),
        in_specs=[cm if cm_in else nat, _resident((1, D_MODEL)), _resident(w1.shape),
                  _resident(w3.shape), _resident(w2.shape), _resident((1, D_MODEL))],
        out_specs=cm if cm_out else nat,
        scratch_shapes=[pltpu.VMEM((D_MODEL // LANES, FFN_TM, LANES), F32)] if (cm_in or cm_out) else [],
        compiler_params=_cparams(("parallel", "parallel")),
        name="ffn_final" if final else "ffn",
    )(h, g, w1, w3, w2, fg)


def _store_k_aug(k_ref, kf, n_pair):
    zeros = jnp.zeros((kf.shape[0], LANES), BF16)
    for p in range(n_pair):
        k_ref[:, p * PAIR_W:p * PAIR_W + LANES] = kf[:, p * LANES:(p + 1) * LANES].astype(BF16)
        k_ref[:, p * PAIR_W + LANES:(p + 1) * PAIR_W] = zeros


def _store_k_halo(k_ref):
    lane = lax.broadcasted_iota(jnp.int32, k_ref.shape, 1)
    k_ref[...] = jnp.where(lane % PAIR_W == LANES, NEG, 0.0).astype(BF16)


def _store_v_aug(v_ref, vf, n_pair):
    lane = _lane_iota(vf.shape[0])
    lower = lane < HEAD_DIM
    one = jnp.where(lane == HEAD_DIM, 1.0, 0.0).astype(F32)
    for p in range(n_pair):
        vp = vf[:, p * LANES:(p + 1) * LANES]
        v_ref[:, p * PAIR_W:p * PAIR_W + LANES] = jnp.where(lower, vp, one).astype(BF16)
        v_ref[:, p * PAIR_W + LANES:(p + 1) * PAIR_W] = (
            jnp.where(lower, pltpu.roll(vp, HEAD_DIM, 1), one).astype(BF16))


def _group_mean_sq(x, gmat):
    ss = x * x
    hi = ss.astype(BF16)
    lo = (ss - hi.astype(F32)).astype(BF16)
    return (_dot(hi, gmat) + _dot(lo, gmat)) * (1.0 / HEAD_DIM)


def _rope(y, cos, sin_signed):
    first_half = (_lane_iota(y.shape[0]) & (HEAD_DIM // 2)) == 0
    outs = []
    for j in range(y.shape[1] // LANES):
        yb = y[:, j * LANES:(j + 1) * LANES]
        partner = jnp.where(first_half,
                            pltpu.roll(yb, LANES - HEAD_DIM // 2, 1),
                            pltpu.roll(yb, HEAD_DIM // 2, 1))
        outs.append(yb * cos + partner * sin_signed)
    return outs[0] if len(outs) == 1 else jnp.concatenate(outs, axis=1)


def _ab_in_body(h_ref, g_ref, w_ref, qg_ref, kg_ref, cos_ref, sin_ref, gq_ref, gk_ref,
                aq_ref, ak_ref, av_ref, bq_ref, bk_ref, bv_ref, *, nblk):
    i = pl.program_id(2)
    is_halo = jnp.logical_or(i == 0, i == nblk + 1)

    @pl.when(is_halo)
    def _():
        _store_k_halo(ak_ref)
        av_ref[...] = jnp.zeros(av_ref.shape, BF16)

    @pl.when(jnp.logical_not(is_halo))
    def _():
        u = _rms(h_ref[...], g_ref[...]).astype(BF16)
        proj = _dot(u, w_ref[...])
        aq_ref[...] = (proj[:, 0:A_W] * QSCALE).astype(BF16)
        _store_k_aug(ak_ref, proj[:, A_W:2 * A_W], A_HEADS // 2)
        _store_v_aug(av_ref, proj[:, 2 * A_W:3 * A_W], A_HEADS // 2)
        o = 3 * A_W
        cos = cos_ref[...]
        sin = sin_ref[...]
        xq = proj[:, o:o + BQ_W]
        yq = xq * lax.rsqrt(_group_mean_sq(xq, gq_ref[...]) + EPS) * qg_ref[...]
        bq_ref[...] = (_rope(yq, cos, sin) * QSCALE).astype(BF16)
        xk = proj[:, o + BQ_W:o + BQ_W + BKV_W]
        yk = xk * lax.rsqrt(_group_mean_sq(xk, gk_ref[...]) + EPS) * kg_ref[...]
        bk_ref[...] = _rope(yk, cos, sin).astype(BF16)
        _store_v_aug(bv_ref, proj[:, o + BQ_W + BKV_W:o + BQ_W + 2 * BKV_W], B_KV_HEADS // 2)


def _ab_in(h, g, w, qg, kg, cos_t, sin_t, gq, gk):
    bsz, _, tc, _ = h.shape
    nblk = tc // PROJ_TM

    def inner(i):
        return jnp.clip(i - 1, 0, nblk - 1)

    def cm(width):
        return pl.BlockSpec((None, None, PROJ_TM, width), lambda b, c, i: (b, c, inner(i), 0))

    def cm_halo(width):
        return pl.BlockSpec((None, None, PROJ_TM, width), lambda b, c, i: (b, c, i, 0))

    tab = pl.BlockSpec((None, PROJ_TM, LANES), lambda b, c, i: (c, inner(i), 0))
    sds = jax.ShapeDtypeStruct
    a_aug = (A_HEADS // 2) * PAIR_W
    return pl.pallas_call(
        functools.partial(_ab_in_body, nblk=nblk),
        out_shape=(sds((bsz, N_CLASS, tc, A_W), BF16),
                   sds((bsz, N_CLASS, tc + 2 * PROJ_TM, a_aug), BF16),
                   sds((bsz, N_CLASS, tc + 2 * PROJ_TM, a_aug), BF16),
                   sds((bsz, N_CLASS, tc, BQ_W), BF16),
                   sds((bsz, N_CLASS, tc, BKV_W), BF16),
                   sds((bsz, N_CLASS, tc, PAIR_W), BF16)),
        grid=(bsz, N_CLASS, nblk + 2),
        in_specs=[cm(D_MODEL), _resident((1, D_MODEL)), _resident(w.shape), _resident((1, BQ_W)),
                  _resident((1, BKV_W)), tab, tab, _resident(gq.shape), _resident(gk.shape)],
        out_specs=(cm(A_W), cm_halo(a_aug), cm_halo(a_aug), cm(BQ_W), cm(BKV_W), cm(PAIR_W)),
        compiler_params=_cparams(("parallel", "parallel", "arbitrary")),
        name="ab_in_proj",
    )(h, g, w, qg, kg, cos_t, sin_t, gq, gk)


def _aug_q(q, keep):
    e0 = jnp.where(_lane_iota(q.shape[0]) == 0, 1.0, 0.0).astype(BF16)
    return jnp.concatenate([jnp.where(keep, q, jnp.zeros_like(q)), e0], axis=1)


def _attn_a_body(q_ref, k_ref, v_ref, ts_ref, to_ref, o_ref):
    u0 = pl.program_id(2) * A_TQ
    lower = _lane_iota(A_TQ) < HEAD_DIM
    upper = jnp.logical_not(lower)
    s_start = pl.multiple_of(u0 + (PROJ_TM - 256), A_TQ)
    o_start = pl.multiple_of(u0 + (PROJ_TM - A_OTH_OFF), 32)

    def windows(c):
        wins = [(c, s_start, A_SAME_W, ts_ref[...])]
        for cp in range(N_CLASS):
            if cp != c:
                wins.append((cp, o_start, A_OTH_W, to_ref[cp - c + (N_CLASS - 1)]))
        return wins

    def scores(c):
        q = q_ref[c]
        qp = jnp.concatenate([_aug_q(q, lower), _aug_q(q, upper)], axis=0)
        return [_dot_nt(k_ref[cp, pl.ds(st, w), :], qp) + tab for cp, st, w, tab in windows(c)]

    def probs(ss):
        m = jnp.max(ss[0], axis=0, keepdims=True)
        for s in ss[1:]:
            m = jnp.maximum(m, jnp.max(s, axis=0, keepdims=True))
        return [jnp.exp2(s - m).astype(BF16) for s in ss]

    def output(c, ps):
        acc = None
        for (cp, st, w, _), p in zip(windows(c), ps):
            d = _dot_tn(v_ref[cp, pl.ds(st, w), :], p)
            acc = d if acc is None else acc + d
        lo = acc[0:HEAD_DIM, 0:A_TQ] / acc[HEAD_DIM:HEAD_DIM + 1, 0:A_TQ]
        hi = acc[LANES:LANES + HEAD_DIM, A_TQ:] / acc[LANES + HEAD_DIM:LANES + HEAD_DIM + 1, A_TQ:]
        o_ref[c] = jnp.concatenate([lo, hi], axis=0).T.astype(o_ref.dtype)

    ss, ps = {}, {}
    for c in range(N_CLASS + 2):
        if c < N_CLASS:
            ss[c] = scores(c)
        if 1 <= c <= N_CLASS:
            ps[c - 1] = probs(ss.pop(c - 1))
        if c >= 2:
            output(c - 2, ps.pop(c - 2))


def _attn_a(aq, ak, av, tab_same, tab_oth):
    bsz, _, tc, _ = aq.shape
    qspec = pl.BlockSpec((None, N_CLASS, A_TQ, LANES), lambda b, p, i: (b, 0, i, p))
    kvspec = pl.BlockSpec((None, N_CLASS, tc + 2 * PROJ_TM, PAIR_W), lambda b, p, i: (b, 0, 0, p))
    return pl.pallas_call(
        _attn_a_body,
        out_shape=jax.ShapeDtypeStruct(aq.shape, BF16),
        grid=(bsz, A_HEADS // 2, tc // A_TQ),
        in_specs=[qspec, kvspec, kvspec,
                  pl.BlockSpec((None, A_SAME_W, 2 * A_TQ), lambda b, p, i: (p, 0, 0)),
                  pl.BlockSpec((None, 2 * N_CLASS - 1, A_OTH_W, 2 * A_TQ), lambda b, p, i: (p, 0, 0, 0))],
        out_specs=qspec,
        compiler_params=_cparams(("parallel", "parallel", "arbitrary")),
        name="attn_dilated",
    )(aq, ak, av, tab_same, tab_oth)


def _attn_b_body(q_ref, k_ref, v_ref, o_ref, qp_ref, m_ref, acc_ref, *, t):
    grp = B_Q_HEADS // B_KV_HEADS
    lower = _lane_iota(B_TQ) < HEAD_DIM
    zero = jnp.zeros((B_TQ, LANES), BF16)
    for j in range(grp):
        qb = q_ref[:, j * LANES:(j + 1) * LANES]
        qp_ref[0, j * B_TQ:(j + 1) * B_TQ, :] = jnp.where(lower, qb, zero)
        qp_ref[1, j * B_TQ:(j + 1) * B_TQ, :] = jnp.where(lower, zero, qb)
    m_ref[...] = jnp.full(m_ref.shape, NEG, F32)
    acc_ref[...] = jnp.zeros(acc_ref.shape, F32)
    n_chunk = B_KV_HEADS * grp

    def step(kb, carry):
        start = pl.multiple_of(kb * B_TK, B_TK)
        k = k_ref[pl.ds(start, B_TK), :]
        v = v_ref[pl.ds(start, B_TK), :]

        def cols(c):
            g, j = divmod(c, grp)
            return g, slice(j * B_TQ, (j + 1) * B_TQ)

        def qk(c):
            g, cs = cols(c)
            return _dot_nt(k, qp_ref[g, cs, :])

        def softmax(c, s):
            g, cs = cols(c)
            m_old = m_ref[g, :, cs]
            m_new = jnp.maximum(m_old, jnp.max(s, axis=0, keepdims=True))
            m_ref[g, :, cs] = m_new
            return jnp.exp2(s - m_new).astype(BF16), jnp.exp2(m_old - m_new)

        def pv(c, p, alpha):
            g, cs = cols(c)
            acc_ref[g, :, cs] = alpha * acc_ref[g, :, cs] + _dot_tn(v[:, g * LANES:(g + 1) * LANES], p)

        ss, pp = {}, {}
        for c in range(n_chunk + B_LAG_PV):
            if c < n_chunk:
                ss[c] = qk(c)
            if B_LAG_SOFTMAX <= c < n_chunk + B_LAG_SOFTMAX:
                pp[c - B_LAG_SOFTMAX] = softmax(c - B_LAG_SOFTMAX, ss.pop(c - B_LAG_SOFTMAX))
            if c >= B_LAG_PV:
                pv(c - B_LAG_PV, *pp.pop(c - B_LAG_PV))
        return carry

    lax.fori_loop(0, t // B_TK, step, 0)
    for j in range(grp):
        cs = slice(j * B_TQ, (j + 1) * B_TQ)
        x = jnp.concatenate([acc_ref[g, 0:HEAD_DIM, cs] / acc_ref[g, HEAD_DIM:HEAD_DIM + 1, cs]
                             for g in range(B_KV_HEADS)], axis=0)
        o_ref[:, j * LANES:(j + 1) * LANES] = x.T.astype(o_ref.dtype)


def _attn_b(bq, bk, bv):
    bsz, t, _ = bq.shape
    rows = (B_Q_HEADS // B_KV_HEADS) * B_TQ
    qspec = pl.BlockSpec((None, B_TQ, BQ_W), lambda b, i: (b, i, 0))
    return pl.pallas_call(
        functools.partial(_attn_b_body, t=t),
        out_shape=jax.ShapeDtypeStruct(bq.shape, BF16),
        grid=(bsz, t // B_TQ),
        in_specs=[qspec,
                  pl.BlockSpec((None, t, BKV_W), lambda b, i: (b, 0, 0)),
                  pl.BlockSpec((None, t, PAIR_W), lambda b, i: (b, 0, 0))],
        out_specs=qspec,
        scratch_shapes=[pltpu.VMEM((B_KV_HEADS, rows, LANES), BF16),
                        pltpu.VMEM((B_KV_HEADS, 1, rows), F32),
                        pltpu.VMEM((B_KV_HEADS, LANES, rows), F32)],
        compiler_params=_cparams(("parallel", "arbitrary")),
        name="attn_global",
    )(bq, bk, bv)


def _ab_out_body(h_ref, a_ref, b_ref, wa_ref, wb_ref, o_ref):
    o_ref[...] = h_ref[...] + _dot(a_ref[...], wa_ref[...]) + _dot(b_ref[...], wb_ref[...])


def _ab_out(h, a_out, b_out, wa, wb):
    bsz, _, tc, _ = h.shape

    def cm(width):
        return pl.BlockSpec((None, None, PROJ_TM, width), lambda b, c, i: (b, c, i, 0))

    return pl.pallas_call(
        _ab_out_body,
        out_shape=jax.ShapeDtypeStruct(h.shape, F32),
        grid=(bsz, N_CLASS, tc // PROJ_TM),
        in_specs=[cm(D_MODEL), cm(A_W), cm(BQ_W), _resident(wa.shape), _resident(wb.shape)],
        out_specs=cm(D_MODEL),
        compiler_params=_cparams(("parallel", "parallel", "parallel")),
        name="ab_out_proj",
    )(h, a_out, b_out, wa, wb)


def _c_in_body(h_ref, g_ref, w_ref, q_ref, k_ref, v_ref, *, nblk):
    i = pl.program_id(1)
    is_halo = jnp.logical_or(i == 0, i == nblk + 1)

    @pl.when(is_halo)
    def _():
        _store_k_halo(k_ref)
        v_ref[...] = jnp.zeros(v_ref.shape, BF16)

    @pl.when(jnp.logical_not(is_halo))
    def _():
        u = _rms(h_ref[...], g_ref[...]).astype(BF16)
        proj = _dot(u, w_ref[...])
        q_ref[...] = (proj[:, 0:CQ_W] * QSCALE).astype(BF16)
        _store_k_aug(k_ref, proj[:, CQ_W:CQ_W + CKV_W], C_KV_HEADS // 2)
        _store_v_aug(v_ref, proj[:, CQ_W + CKV_W:CQ_W + 2 * CKV_W], C_KV_HEADS // 2)


def _c_in(h, g, w):
    bsz, t, _ = h.shape
    nblk = t // PROJ_TM
    aug = (C_KV_HEADS // 2) * PAIR_W

    def inner(i):
        return jnp.clip(i - 1, 0, nblk - 1)

    sds = jax.ShapeDtypeStruct
    return pl.pallas_call(
        functools.partial(_c_in_body, nblk=nblk),
        out_shape=(sds((bsz, t, CQ_W), BF16),
                   sds((bsz, t + 2 * PROJ_TM, aug), BF16),
                   sds((bsz, t + 2 * PROJ_TM, aug), BF16)),
        grid=(bsz, nblk + 2),
        in_specs=[pl.BlockSpec((None, PROJ_TM, D_MODEL), lambda b, i: (b, inner(i), 0)),
                  _resident((1, D_MODEL)), _resident(w.shape)],
        out_specs=(pl.BlockSpec((None, PROJ_TM, CQ_W), lambda b, i: (b, inner(i), 0)),
                   pl.BlockSpec((None, PROJ_TM, aug), lambda b, i: (b, i, 0)),
                   pl.BlockSpec((None, PROJ_TM, aug), lambda b, i: (b, i, 0))),
        compiler_params=_cparams(("parallel", "arbitrary")),
        name="c_in_proj",
    )(h, g, w)


def _attn_c_body(q_ref, k_ref, v_ref, tab_ref, sink_ref, o_ref):
    t0 = pl.program_id(2) * C_TQ
    grp = C_Q_HEADS // C_KV_HEADS
    lower = _lane_iota(C_TQ) < HEAD_DIM
    start = pl.multiple_of(t0 + (PROJ_TM - C_TQ), C_TQ)
    k = k_ref[pl.ds(start, C_KW), :]
    v = v_ref[pl.ds(start, C_KW), :]
    n_chunk = grp

    def cols(ch):
        return slice(ch * 2 * C_TQ, (ch + 1) * 2 * C_TQ)

    def scores(ch):
        half, js = ch // 2, (2 * (ch % 2), 2 * (ch % 2) + 1)
        keep = lower if half == 0 else jnp.logical_not(lower)
        qp = jnp.concatenate([_aug_q(q_ref[:, j * LANES:(j + 1) * LANES], keep) for j in js], axis=0)
        return _dot_nt(k, qp) + tab_ref[:, cols(ch)]

    def probs(ch, s):
        sink = sink_ref[:, cols(ch)]
        m = jnp.maximum(jnp.max(s, axis=0, keepdims=True), sink)
        return jnp.exp2(s - m).astype(BF16), jnp.exp2(sink - m)

    def output(ch, p, sink_p):
        half = ch // 2
        acc = _dot_tn(v[:, half * LANES:(half + 1) * LANES], p)
        o = acc[0:HEAD_DIM] / (acc[HEAD_DIM:HEAD_DIM + 1] + sink_p)
        return [o[:, 0:C_TQ], o[:, C_TQ:]]

    heads = []
    ss, ps = {}, {}
    for ch in range(n_chunk + 2):
        if ch < n_chunk:
            ss[ch] = scores(ch)
        if 1 <= ch <= n_chunk:
            ps[ch - 1] = probs(ch - 1, ss.pop(ch - 1))
        if ch >= 2:
            heads += output(ch - 2, *ps.pop(ch - 2))
    for j in range(grp):
        x = jnp.concatenate([heads[j], heads[grp + j]], axis=0)
        o_ref[:, j * LANES:(j + 1) * LANES] = x.T.astype(o_ref.dtype)


def _attn_c(cq, ck, cv, tab, sink_row):
    bsz, t, _ = cq.shape
    n_pair = C_KV_HEADS // 2
    cols = (C_Q_HEADS // n_pair) * C_TQ
    qspec = pl.BlockSpec((None, C_TQ, CQ_W // n_pair), lambda b, p, i: (b, i, p))
    kvspec = pl.BlockSpec((None, t + 2 * PROJ_TM, PAIR_W), lambda b, p, i: (b, 0, p))
    return pl.pallas_call(
        _attn_c_body,
        out_shape=jax.ShapeDtypeStruct(cq.shape, BF16),
        grid=(bsz, n_pair, t // C_TQ),
        in_specs=[qspec, kvspec, kvspec,
                  pl.BlockSpec((None, C_KW, cols), lambda b, p, i: (p, 0, 0)),
                  pl.BlockSpec((None, 1, cols), lambda b, p, i: (p, 0, 0))],
        out_specs=qspec,
        compiler_params=_cparams(("parallel", "parallel", "arbitrary")),
        name="attn_banded",
    )(cq, ck, cv, tab, sink_row)


def _c_out_body(h_ref, a_ref, w_ref, o_ref):
    o_ref[...] = h_ref[...] + _dot(a_ref[...], w_ref[...])


def _c_out(h, a, w):
    bsz, t, _ = h.shape
    row = lambda width: pl.BlockSpec((None, FFN_TM, width), lambda b, i: (b, i, 0))
    return pl.pallas_call(
        _c_out_body,
        out_shape=jax.ShapeDtypeStruct(h.shape, F32),
        grid=(bsz, t // FFN_TM),
        in_specs=[row(D_MODEL), row(CQ_W), _resident(w.shape)],
        out_specs=row(D_MODEL),
        compiler_params=_cparams(("parallel", "parallel")),
        name="c_out_proj",
    )(h, a, w)


def _alibi_slopes(n):
    return (2.0 ** (-8.0 * (np.arange(n) + 1) / n)).astype(np.float32).astype(np.float64)


def _dilated_tables():
    slopes = _alibi_slopes(A_HEADS)[:, None, None]

    def table(off):
        mult = np.zeros(off.shape)
        for w, d in A_BRANCHES:
            mult += (off % d == 0) & (np.abs(off) <= w // 2)
        bias = (-slopes * np.abs(off)[None] + np.log(np.maximum(mult, 1.0))[None]) * LOG2E
        tab = np.where(mult[None] > 0, bias, NEG).astype(np.float32)
        tab = tab.reshape(A_HEADS // 2, 2, off.shape[0], A_TQ).transpose(0, 2, 1, 3)
        return tab.reshape(A_HEADS // 2, off.shape[0], 2 * A_TQ)

    qi = np.arange(A_TQ)[None, :]
    same = table(N_CLASS * (np.arange(A_SAME_W)[:, None] - 256 - qi))
    oth = []
    for delta in range(-(N_CLASS - 1), N_CLASS):
        off = N_CLASS * (np.arange(A_OTH_W)[:, None] - A_OTH_OFF - qi) + delta
        oth.append(table(off) if delta != 0 else np.full((A_HEADS // 2, A_OTH_W, 2 * A_TQ), NEG, np.float32))
    return jnp.asarray(same), jnp.asarray(np.stack(oth, axis=1))


def _c_head(p, half, j):
    grp = C_Q_HEADS // C_KV_HEADS
    return 2 * grp * p + grp * half + j


def _c_head_order():
    grp = C_Q_HEADS // C_KV_HEADS
    return [_c_head(p, half, j) for p in range(C_KV_HEADS // 2) for j in range(grp) for half in range(2)]


def _banded_table():
    slopes = _alibi_slopes(C_Q_HEADS)
    grp = C_Q_HEADS // C_KV_HEADS
    rel = (np.arange(C_KW)[:, None] - C_TQ) - np.arange(C_TQ)[None, :]
    tab = np.zeros((C_KV_HEADS // 2, C_KW, 2 * grp, C_TQ), np.float32)
    for p in range(C_KV_HEADS // 2):
        for half in range(2):
            for j in range(grp):
                bias = -slopes[_c_head(p, half, j)] * np.abs(rel) * LOG2E
                tab[p, :, half * grp + j, :] = np.where(np.abs(rel) <= C_WINDOW, bias, NEG)
    return jnp.asarray(tab.reshape(C_KV_HEADS // 2, C_KW, 2 * grp * C_TQ))


def _rope_tables(t):
    n_rows = t // GRID_W
    half = HEAD_DIM // 2
    row = jnp.repeat(jnp.arange(n_rows), GRID_W).astype(F32)
    col = jnp.tile(jnp.arange(GRID_W), n_rows).astype(F32)
    fr = ROPE_THETA ** (-jnp.arange(0, half, 2, dtype=F32) / half)
    ang = jnp.concatenate([row[:, None] * fr, col[:, None] * fr], axis=-1)
    cos, sin = jnp.cos(ang), jnp.sin(ang)
    cos_h = jnp.concatenate([cos, cos], axis=-1)
    sin_h = jnp.concatenate([-sin, sin], axis=-1)

    def cm(x):
        x = jnp.concatenate([x, x], axis=-1)
        return x.reshape(t // N_CLASS, N_CLASS, LANES).transpose(1, 0, 2)

    return cm(cos_h), cm(sin_h)


_DEINT = np.concatenate([np.arange(0, HEAD_DIM, 2), np.arange(1, HEAD_DIM, 2)])
_B_HEAD_ORDER = [0, 4, 1, 5, 2, 6, 3, 7]


def _head_cols(heads, base, dims):
    return np.concatenate([base + h * HEAD_DIM + dims for h in heads])


def _prep_ab_weights(w_in, w_out, q_gain, k_gain):
    nat = np.arange(HEAD_DIM)
    o = 3 * A_W
    cols = np.concatenate([
        np.arange(o),
        _head_cols(_B_HEAD_ORDER, o, _DEINT),
        _head_cols(range(B_KV_HEADS), o + BQ_W, _DEINT),
        np.arange(o + BQ_W + BKV_W, o + BQ_W + 2 * BKV_W)])
    w = w_in[:, cols].astype(BF16)
    wa = w_out[:A_W].astype(BF16)
    wb = w_out[_head_cols(_B_HEAD_ORDER, A_W, nat)].astype(BF16)
    qg = jnp.tile(q_gain[_DEINT], B_Q_HEADS)[None, :]
    kg = jnp.tile(k_gain[_DEINT], B_KV_HEADS)[None, :]
    return w, wa, wb, qg, kg


def _prep_c_weights(w_in, w_out, sink):
    nat = np.arange(HEAD_DIM)
    qcols = _head_cols(_c_head_order(), 0, nat)
    cols = np.concatenate([qcols, np.arange(CQ_W, CQ_W + 2 * CKV_W)])
    w = w_in[:, cols].astype(BF16)
    wo = w_out[qcols].astype(BF16)
    grp = C_Q_HEADS // C_KV_HEADS
    idx = np.array([[_c_head(p, half, j) for half in range(2) for j in range(grp)]
                    for p in range(C_KV_HEADS // 2)])
    sink_row = jnp.repeat(sink[idx] * LOG2E, C_TQ, axis=-1)[:, None, :].astype(F32)
    return w, wo, sink_row


def _block_ones(width):
    g = np.arange(width) // HEAD_DIM
    return jnp.asarray((g[:, None] == g[None, :]).astype(np.float32)).astype(BF16)


def _trunk(x, p):
    bsz, t, _ = x.shape
    tc = t // N_CLASS
    cos_t, sin_t = _rope_tables(t)

    h = _ffn(x, p["g"][0][0], *p["ffn"][0][0], p["fg"], cm_out=True)
    aq, ak, av, bq, bk, bv = _ab_in(h, p["g"][0][1], p["ab_w"], p["qg"], p["kg"],
                                    cos_t, sin_t, p["gq"], p["gk"])
    a_out = _attn_a(aq, ak, av, p["tab_same"], p["tab_oth"])
    b_out = _attn_b(bq.reshape(bsz, t, BQ_W), bk.reshape(bsz, t, BKV_W), bv.reshape(bsz, t, PAIR_W))
    h = _ab_out(h, a_out, b_out.reshape(bsz, N_CLASS, tc, BQ_W), p["ab_wa"], p["ab_wb"])
    h = _ffn(h, p["g"][0][2], *p["ffn"][0][1], p["fg"], cm_in=True)

    h = _ffn(h, p["g"][1][0], *p["ffn"][1][0], p["fg"])
    cq, ck, cv = _c_in(h, p["g"][1][1], p["c_w"])
    c_out = _attn_c(cq, ck, cv, p["tab_c"], p["sink"])
    h = _c_out(h, c_out, p["c_wo"])
    return _ffn(h, p["g"][1][2], *p["ffn"][1][1], p["fg"], final=True)


def kernel(x_prompt, x_sample, norm_g, ffn_w1, ffn_w3, ffn_w2, ab_w_in, ab_w_out, ab_q_gain, ab_k_gain,
           c_w_in, c_w_out, c_sink, final_g):
    tab_same, tab_oth = _dilated_tables()
    ab_w, ab_wa, ab_wb, qg, kg = _prep_ab_weights(ab_w_in[0], ab_w_out[0], ab_q_gain[0], ab_k_gain[0])
    c_w, c_wo, sink = _prep_c_weights(c_w_in[0], c_w_out[0], c_sink[0])
    p = {
        "g": [[norm_g[l, k][None, :] for k in range(3)] for l in range(2)],
        "ffn": [[(ffn_w1[l, k].astype(BF16), ffn_w3[l, k].astype(BF16), ffn_w2[l, k].astype(BF16))
                 for k in range(2)] for l in range(2)],
        "fg": final_g[None, :],
        "ab_w": ab_w, "ab_wa": ab_wa, "ab_wb": ab_wb, "qg": qg, "kg": kg,
        "gq": _block_ones(BQ_W), "gk": _block_ones(BKV_W),
        "tab_same": tab_same, "tab_oth": tab_oth,
        "c_w": c_w, "c_wo": c_wo, "sink": sink, "tab_c": _banded_table(),
    }
    return _trunk(x_prompt, p), _trunk(x_sample, p)
```

```python
import functools

import numpy as np
import jax
import jax.numpy as jnp
from jax import lax
from jax.experimental import pallas as pl
from jax.experimental.pallas import tpu as pltpu

F32 = jnp.float32
BF16 = jnp.bfloat16

D_MODEL = 1024
HEAD_DIM = 64
D_FF = 2816
EPS = 1e-6
NEG = -1e30
GRID_W = 64
ROPE_THETA = 10000.0
A_HEADS = 8
A_BRANCHES = ((128, 1), (512, 4), (2048, 16))
B_Q_HEADS = 8
B_KV_HEADS = 2
C_Q_HEADS = 16
C_KV_HEADS = 4
C_WINDOW = 128
A_W = A_HEADS * HEAD_DIM
BQ_W = B_Q_HEADS * HEAD_DIM
BKV_W = B_KV_HEADS * HEAD_DIM
CQ_W = C_Q_HEADS * HEAD_DIM
CKV_W = C_KV_HEADS * HEAD_DIM
LOG2E = 1.4426950408889634
QSCALE = HEAD_DIM ** -0.5 * LOG2E

LANES = 128
PAIR_W = 2 * LANES
N_CLASS = 4
VMEM_LIMIT = 56 * 1024 * 1024

FFN_TM = 512
FF_CHUNK = 256
PROJ_TM = 256
A_TQ = 128
A_SAME_W = 640
A_OTH_W = 192
A_OTH_OFF = 32
B_TQ = 256
B_TK = 512
B_LAG_PV = 1
C_TQ = 128
C_KW = 3 * C_TQ


def _cparams(sem):
    return pltpu.CompilerParams(dimension_semantics=sem, vmem_limit_bytes=VMEM_LIMIT)


def _rms(x, g):
    return x * lax.rsqrt(jnp.mean(x * x, axis=-1, keepdims=True) + EPS) * g


def _dot(a, b):
    return jnp.dot(a, b, preferred_element_type=F32)


def _dot_nt(a, b):
    return lax.dot_general(a, b, (((1,), (1,)), ((), ())), preferred_element_type=F32)


def _dot_tn(a, b):
    return lax.dot_general(a, b, (((0,), (0,)), ((), ())), preferred_element_type=F32)


def _resident(shape):
    nd = len(shape)
    return pl.BlockSpec(shape, lambda *_: (0,) * nd, pipeline_mode=pl.Buffered(1))


def _lane_iota(rows):
    return lax.broadcasted_iota(jnp.int32, (rows, LANES), 1)


def _ffn_body(*refs, final, cm_in, cm_out, n_mix):
    h_ref = refs[0]
    mix_refs = refs[1:1 + n_mix]
    mixw_refs = refs[1 + n_mix:1 + 2 * n_mix]
    g_ref, w1_ref, w3_ref, w2_ref, fg_ref, o_ref = refs[1 + 2 * n_mix:7 + 2 * n_mix]
    scratch = refs[7 + 2 * n_mix:]
    sub = FFN_TM // N_CLASS
    n_col = D_MODEL // LANES
    x = h_ref[...].reshape(FFN_TM, D_MODEL) if cm_in else h_ref[...]
    for a_ref, w_ref in zip(mix_refs, mixw_refs):
        a = a_ref[...]
        x = x + _dot(a.reshape(FFN_TM, a.shape[-1]) if cm_in else a, w_ref[...])
    n = _rms(x, g_ref[...]).astype(BF16)
    acc = jnp.zeros(x.shape, F32)
    for c in range(D_FF // FF_CHUNK):
        sl = slice(c * FF_CHUNK, (c + 1) * FF_CHUNK)
        a = _dot(n, w1_ref[:, sl])
        b = _dot(n, w3_ref[:, sl])
        hm = (a * jax.nn.sigmoid(a) * b).astype(BF16)
        acc = acc + _dot(hm, w2_ref[sl, :])
    y = x + 0.5 * acc
    if final:
        y = _rms(y, fg_ref[...])
    if cm_out:
        perm_ref, = scratch
        for j in range(n_col):
            perm_ref[j] = y[:, j * LANES:(j + 1) * LANES]
        for c in range(N_CLASS):
            for j in range(n_col):
                o_ref[c, :, j * LANES:(j + 1) * LANES] = perm_ref[j, pl.ds(c, sub, stride=N_CLASS), :]
    elif cm_in:
        perm_ref, = scratch
        for c in range(N_CLASS):
            for j in range(n_col):
                perm_ref[j, pl.ds(c, sub, stride=N_CLASS), :] = y[c * sub:(c + 1) * sub, j * LANES:(j + 1) * LANES]
        o_ref[...] = jnp.concatenate([perm_ref[j] for j in range(n_col)], axis=1)
    else:
        o_ref[...] = y


def _ffn(h, mix, g, w1, w3, w2, fg, *, final=False, cm_in=False, cm_out=False):
    bsz = h.shape[0]
    t = h.shape[1] * h.shape[2] if cm_in else h.shape[1]
    tc = t // N_CLASS
    sub = FFN_TM // N_CLASS

    def rows(width, class_major):
        if class_major:
            return pl.BlockSpec((None, N_CLASS, sub, width), lambda b, i: (b, 0, i, 0))
        return pl.BlockSpec((None, FFN_TM, width), lambda b, i: (b, i, 0))

    acts = [a for a, _ in mix]
    ws = [w for _, w in mix]
    out_shape = (bsz, N_CLASS, tc, D_MODEL) if cm_out else (bsz, t, D_MODEL)
    return pl.pallas_call(
        functools.partial(_ffn_body, final=final, cm_in=cm_in, cm_out=cm_out, n_mix=len(mix)),
        out_shape=jax.ShapeDtypeStruct(out_shape, F32),
        grid=(bsz, t // FFN_TM),
        in_specs=([rows(D_MODEL, cm_in)] + [rows(a.shape[-1], cm_in) for a in acts]
                  + [_resident(w.shape) for w in ws]
                  + [_resident((1, D_MODEL)), _resident(w1.shape), _resident(w3.shape),
                     _resident(w2.shape), _resident((1, D_MODEL))]),
        out_specs=rows(D_MODEL, cm_out),
        scratch_shapes=[pltpu.VMEM((D_MODEL // LANES, FFN_TM, LANES), F32)] if (cm_in or cm_out) else [],
        compiler_params=_cparams(("parallel", "parallel")),
        name="ffn_final" if final else "ffn",
    )(h, *acts, *ws, g, w1, w3, w2, fg)


def _store_k_aug(k_ref, kf, n_pair):
    zeros = jnp.zeros((kf.shape[0], LANES), BF16)
    for p in range(n_pair):
        k_ref[:, p * PAIR_W:p * PAIR_W + LANES] = kf[:, p * LANES:(p + 1) * LANES].astype(BF16)
        k_ref[:, p * PAIR_W + LANES:(p + 1) * PAIR_W] = zeros


def _store_k_halo(k_ref):
    lane = lax.broadcasted_iota(jnp.int32, k_ref.shape, 1)
    k_ref[...] = jnp.where(lane % PAIR_W == LANES, NEG, 0.0).astype(BF16)


def _store_v_aug(v_ref, vf, n_pair):
    lane = _lane_iota(vf.shape[0])
    lower = lane < HEAD_DIM
    one = jnp.where(lane == HEAD_DIM, 1.0, 0.0).astype(F32)
    for p in range(n_pair):
        vp = vf[:, p * LANES:(p + 1) * LANES]
        v_ref[:, p * PAIR_W:p * PAIR_W + LANES] = jnp.where(lower, vp, one).astype(BF16)
        v_ref[:, p * PAIR_W + LANES:(p + 1) * PAIR_W] = (
            jnp.where(lower, pltpu.roll(vp, HEAD_DIM, 1), one).astype(BF16))


def _group_mean_sq(x, gmat):
    ss = x * x
    hi = ss.astype(BF16)
    lo = (ss - hi.astype(F32)).astype(BF16)
    return (_dot(hi, gmat) + _dot(lo, gmat)) * (1.0 / HEAD_DIM)


def _rope(y, cos, sin_signed):
    first_half = (_lane_iota(y.shape[0]) & (HEAD_DIM // 2)) == 0
    outs = []
    for j in range(y.shape[1] // LANES):
        yb = y[:, j * LANES:(j + 1) * LANES]
        partner = jnp.where(first_half,
                            pltpu.roll(yb, LANES - HEAD_DIM // 2, 1),
                            pltpu.roll(yb, HEAD_DIM // 2, 1))
        outs.append(yb * cos + partner * sin_signed)
    return outs[0] if len(outs) == 1 else jnp.concatenate(outs, axis=1)


def _ab_in_body(h_ref, g_ref, w_ref, qg_ref, kg_ref, cos_ref, sin_ref, gq_ref, gk_ref,
                aq_ref, ak_ref, av_ref, bq_ref, bk_ref, bv_ref, *, nblk):
    i = pl.program_id(2)
    is_halo = jnp.logical_or(i == 0, i == nblk + 1)

    @pl.when(is_halo)
    def _():
        _store_k_halo(ak_ref)
        av_ref[...] = jnp.zeros(av_ref.shape, BF16)

    @pl.when(jnp.logical_not(is_halo))
    def _():
        u = _rms(h_ref[...], g_ref[...]).astype(BF16)
        proj = _dot(u, w_ref[...])
        aq_ref[...] = (proj[:, 0:A_W] * QSCALE).astype(BF16)
        _store_k_aug(ak_ref, proj[:, A_W:2 * A_W], A_HEADS // 2)
        _store_v_aug(av_ref, proj[:, 2 * A_W:3 * A_W], A_HEADS // 2)
        o = 3 * A_W
        cos = cos_ref[...]
        sin = sin_ref[...]
        xq = proj[:, o:o + BQ_W]
        yq = xq * lax.rsqrt(_group_mean_sq(xq, gq_ref[...]) + EPS) * qg_ref[...]
        bq_ref[...] = (_rope(yq, cos, sin) * QSCALE).astype(BF16)
        xk = proj[:, o + BQ_W:o + BQ_W + BKV_W]
        yk = xk * lax.rsqrt(_group_mean_sq(xk, gk_ref[...]) + EPS) * kg_ref[...]
        bk_ref[...] = _rope(yk, cos, sin).astype(BF16)
        _store_v_aug(bv_ref, proj[:, o + BQ_W + BKV_W:o + BQ_W + 2 * BKV_W], B_KV_HEADS // 2)


def _ab_in(h, g, w, qg, kg, cos_t, sin_t, gq, gk):
    bsz, _, tc, _ = h.shape
    nblk = tc // PROJ_TM

    def inner(i):
        return jnp.clip(i - 1, 0, nblk - 1)

    def cm(width):
        return pl.BlockSpec((None, None, PROJ_TM, width), lambda b, c, i: (b, c, inner(i), 0))

    def cm_halo(width):
        return pl.BlockSpec((None, None, PROJ_TM, width), lambda b, c, i: (b, c, i, 0))

    tab = pl.BlockSpec((None, PROJ_TM, LANES), lambda b, c, i: (c, inner(i), 0))
    sds = jax.ShapeDtypeStruct
    a_aug = (A_HEADS // 2) * PAIR_W
    return pl.pallas_call(
        functools.partial(_ab_in_body, nblk=nblk),
        out_shape=(sds((bsz, N_CLASS, tc, A_W), BF16),
                   sds((bsz, N_CLASS, tc + 2 * PROJ_TM, a_aug), BF16),
                   sds((bsz, N_CLASS, tc + 2 * PROJ_TM, a_aug), BF16),
                   sds((bsz, N_CLASS, tc, BQ_W), BF16),
                   sds((bsz, N_CLASS, tc, BKV_W), BF16),
                   sds((bsz, N_CLASS, tc, PAIR_W), BF16)),
        grid=(bsz, N_CLASS, nblk + 2),
        in_specs=[cm(D_MODEL), _resident((1, D_MODEL)), _resident(w.shape), _resident((1, BQ_W)),
                  _resident((1, BKV_W)), tab, tab, _resident(gq.shape), _resident(gk.shape)],
        out_specs=(cm(A_W), cm_halo(a_aug), cm_halo(a_aug), cm(BQ_W), cm(BKV_W), cm(PAIR_W)),
        compiler_params=_cparams(("parallel", "parallel", "arbitrary")),
        name="ab_in_proj",
    )(h, g, w, qg, kg, cos_t, sin_t, gq, gk)


def _aug_q(q, keep):
    e0 = jnp.where(_lane_iota(q.shape[0]) == 0, 1.0, 0.0).astype(BF16)
    return jnp.concatenate([jnp.where(keep, q, jnp.zeros_like(q)), e0], axis=1)


def _attn_a_body(q_ref, k_ref, v_ref, ts_ref, to_ref, o_ref):
    u0 = pl.program_id(2) * A_TQ
    lower = _lane_iota(A_TQ) < HEAD_DIM
    upper = jnp.logical_not(lower)
    s_start = pl.multiple_of(u0 + (PROJ_TM - 256), A_TQ)
    o_start = pl.multiple_of(u0 + (PROJ_TM - A_OTH_OFF), 32)

    def windows(c):
        wins = [(c, s_start, A_SAME_W, ts_ref[...])]
        for cp in range(N_CLASS):
            if cp != c:
                wins.append((cp, o_start, A_OTH_W, to_ref[cp - c + (N_CLASS - 1)]))
        return wins

    def scores(c):
        q = q_ref[c]
        qp = jnp.concatenate([_aug_q(q, lower), _aug_q(q, upper)], axis=0)
        return [_dot_nt(k_ref[cp, pl.ds(st, w), :], qp) + tab for cp, st, w, tab in windows(c)]

    def probs(ss):
        m = jnp.max(ss[0], axis=0, keepdims=True)
        for s in ss[1:]:
            m = jnp.maximum(m, jnp.max(s, axis=0, keepdims=True))
        return [jnp.exp2(s - m).astype(BF16) for s in ss]

    def output(c, ps):
        acc = None
        for (cp, st, w, _), p in zip(windows(c), ps):
            d = _dot_tn(v_ref[cp, pl.ds(st, w), :], p)
            acc = d if acc is None else acc + d
        lo = acc[0:HEAD_DIM, 0:A_TQ] / acc[HEAD_DIM:HEAD_DIM + 1, 0:A_TQ]
        hi = acc[LANES:LANES + HEAD_DIM, A_TQ:] / acc[LANES + HEAD_DIM:LANES + HEAD_DIM + 1, A_TQ:]
        o_ref[c] = jnp.concatenate([lo, hi], axis=0).T.astype(o_ref.dtype)

    ss, ps = {}, {}
    for c in range(N_CLASS + 2):
        if c < N_CLASS:
            ss[c] = scores(c)
        if 1 <= c <= N_CLASS:
            ps[c - 1] = probs(ss.pop(c - 1))
        if c >= 2:
            output(c - 2, ps.pop(c - 2))


def _attn_a(aq, ak, av, tab_same, tab_oth):
    bsz, _, tc, _ = aq.shape
    qspec = pl.BlockSpec((None, N_CLASS, A_TQ, LANES), lambda b, p, i: (b, 0, i, p))
    kvspec = pl.BlockSpec((None, N_CLASS, tc + 2 * PROJ_TM, PAIR_W), lambda b, p, i: (b, 0, 0, p))
    return pl.pallas_call(
        _attn_a_body,
        out_shape=jax.ShapeDtypeStruct(aq.shape, BF16),
        grid=(bsz, A_HEADS // 2, tc // A_TQ),
        in_specs=[qspec, kvspec, kvspec,
                  pl.BlockSpec((None, A_SAME_W, 2 * A_TQ), lambda b, p, i: (p, 0, 0)),
                  pl.BlockSpec((None, 2 * N_CLASS - 1, A_OTH_W, 2 * A_TQ), lambda b, p, i: (p, 0, 0, 0))],
        out_specs=qspec,
        compiler_params=_cparams(("parallel", "parallel", "arbitrary")),
        name="attn_dilated",
    )(aq, ak, av, tab_same, tab_oth)


def _attn_b_body(q_ref, k_ref, v_ref, o_ref, qp_ref, m_ref, acc_ref, s_ref, *, t):
    grp = B_Q_HEADS // B_KV_HEADS
    lower = _lane_iota(B_TQ) < HEAD_DIM
    zero = jnp.zeros((B_TQ, LANES), BF16)
    for j in range(grp):
        qb = q_ref[:, j * LANES:(j + 1) * LANES]
        qp_ref[0, j * B_TQ:(j + 1) * B_TQ, :] = jnp.where(lower, qb, zero)
        qp_ref[1, j * B_TQ:(j + 1) * B_TQ, :] = jnp.where(lower, zero, qb)
    m_ref[...] = jnp.full(m_ref.shape, NEG, F32)
    acc_ref[...] = jnp.zeros(acc_ref.shape, F32)
    n_chunk = B_KV_HEADS * grp
    n_steps = t // B_TK

    def cols(c):
        g, j = divmod(c, grp)
        return g, slice(j * B_TQ, (j + 1) * B_TQ)

    def rows(kb):
        return pl.ds(pl.multiple_of(kb * B_TK, B_TK), B_TK)

    def qk(k, c, slot):
        g, cs = cols(c)
        s_ref[slot, c] = _dot_nt(k, qp_ref[g, cs, :])

    def softmax(c, slot):
        g, cs = cols(c)
        s = s_ref[slot, c]
        m_old = m_ref[g, :, cs]
        m_new = jnp.maximum(m_old, jnp.max(s, axis=0, keepdims=True))
        m_ref[g, :, cs] = m_new
        return jnp.exp2(s - m_new).astype(BF16), jnp.exp2(m_old - m_new)

    def pv(v, c, p, alpha):
        g, cs = cols(c)
        acc_ref[g, :, cs] = alpha * acc_ref[g, :, cs] + _dot_tn(v[:, g * LANES:(g + 1) * LANES], p)

    def block(kb, slot, with_next):
        v = v_ref[rows(kb), :]
        k_next = k_ref[rows(kb + 1), :] if with_next else None
        pp = {}
        for c in range(n_chunk + B_LAG_PV):
            if c < n_chunk:
                pp[c] = softmax(c, slot)
                if with_next:
                    qk(k_next, c, 1 - slot)
            if c >= B_LAG_PV:
                pv(v, c - B_LAG_PV, *pp.pop(c - B_LAG_PV))

    k0 = k_ref[rows(0), :]
    for c in range(n_chunk):
        qk(k0, c, 0)

    def pair(i, carry):
        block(2 * i, 0, True)
        block(2 * i + 1, 1, True)
        return carry

    lax.fori_loop(0, n_steps // 2 - 1, pair, 0)
    block(n_steps - 2, 0, True)
    block(n_steps - 1, 1, False)
    for j in range(grp):
        cs = slice(j * B_TQ, (j + 1) * B_TQ)
        x = jnp.concatenate([acc_ref[g, 0:HEAD_DIM, cs] / acc_ref[g, HEAD_DIM:HEAD_DIM + 1, cs]
                             for g in range(B_KV_HEADS)], axis=0)
        o_ref[:, j * LANES:(j + 1) * LANES] = x.T.astype(o_ref.dtype)


def _attn_b(bq, bk, bv):
    bsz, t, _ = bq.shape
    rows = (B_Q_HEADS // B_KV_HEADS) * B_TQ
    qspec = pl.BlockSpec((None, B_TQ, BQ_W), lambda b, i: (b, i, 0))
    return pl.pallas_call(
        functools.partial(_attn_b_body, t=t),
        out_shape=jax.ShapeDtypeStruct(bq.shape, BF16),
        grid=(bsz, t // B_TQ),
        in_specs=[qspec,
                  pl.BlockSpec((None, t, BKV_W), lambda b, i: (b, 0, 0)),
                  pl.BlockSpec((None, t, PAIR_W), lambda b, i: (b, 0, 0))],
        out_specs=qspec,
        scratch_shapes=[pltpu.VMEM((B_KV_HEADS, rows, LANES), BF16),
                        pltpu.VMEM((B_KV_HEADS, 1, rows), F32),
                        pltpu.VMEM((B_KV_HEADS, LANES, rows), F32),
                        pltpu.VMEM((2, B_Q_HEADS, B_TK, B_TQ), F32)],
        compiler_params=_cparams(("parallel", "arbitrary")),
        name="attn_global",
    )(bq, bk, bv)


def _c_in_body(h_ref, g_ref, w_ref, q_ref, k_ref, v_ref, *, nblk):
    i = pl.program_id(1)
    is_halo = jnp.logical_or(i == 0, i == nblk + 1)

    @pl.when(is_halo)
    def _():
        _store_k_halo(k_ref)
        v_ref[...] = jnp.zeros(v_ref.shape, BF16)

    @pl.when(jnp.logical_not(is_halo))
    def _():
        u = _rms(h_ref[...], g_ref[...]).astype(BF16)
        proj = _dot(u, w_ref[...])
        q_ref[...] = (proj[:, 0:CQ_W] * QSCALE).astype(BF16)
        _store_k_aug(k_ref, proj[:, CQ_W:CQ_W + CKV_W], C_KV_HEADS // 2)
        _store_v_aug(v_ref, proj[:, CQ_W + CKV_W:CQ_W + 2 * CKV_W], C_KV_HEADS // 2)


def _c_in(h, g, w):
    bsz, t, _ = h.shape
    nblk = t // PROJ_TM
    aug = (C_KV_HEADS // 2) * PAIR_W

    def inner(i):
        return jnp.clip(i - 1, 0, nblk - 1)

    sds = jax.ShapeDtypeStruct
    return pl.pallas_call(
        functools.partial(_c_in_body, nblk=nblk),
        out_shape=(sds((bsz, t, CQ_W), BF16),
                   sds((bsz, t + 2 * PROJ_TM, aug), BF16),
                   sds((bsz, t + 2 * PROJ_TM, aug), BF16)),
        grid=(bsz, nblk + 2),
        in_specs=[pl.BlockSpec((None, PROJ_TM, D_MODEL), lambda b, i: (b, inner(i), 0)),
                  _resident((1, D_MODEL)), _resident(w.shape)],
        out_specs=(pl.BlockSpec((None, PROJ_TM, CQ_W), lambda b, i: (b, inner(i), 0)),
                   pl.BlockSpec((None, PROJ_TM, aug), lambda b, i: (b, i, 0)),
                   pl.BlockSpec((None, PROJ_TM, aug), lambda b, i: (b, i, 0))),
        compiler_params=_cparams(("parallel", "arbitrary")),
        name="c_in_proj",
    )(h, g, w)


def _attn_c_body(q_ref, k_ref, v_ref, tab_ref, sink_ref, o_ref):
    t0 = pl.program_id(2) * C_TQ
    grp = C_Q_HEADS // C_KV_HEADS
    lower = _lane_iota(C_TQ) < HEAD_DIM
    start = pl.multiple_of(t0 + (PROJ_TM - C_TQ), C_TQ)
    k = k_ref[pl.ds(start, C_KW), :]
    v = v_ref[pl.ds(start, C_KW), :]
    n_chunk = grp

    def cols(ch):
        return slice(ch * 2 * C_TQ, (ch + 1) * 2 * C_TQ)

    def scores(ch):
        half, js = ch // 2, (2 * (ch % 2), 2 * (ch % 2) + 1)
        keep = lower if half == 0 else jnp.logical_not(lower)
        qp = jnp.concatenate([_aug_q(q_ref[:, j * LANES:(j + 1) * LANES], keep) for j in js], axis=0)
        return _dot_nt(k, qp) + tab_ref[:, cols(ch)]

    def probs(ch, s):
        sink = sink_ref[:, cols(ch)]
        m = jnp.maximum(jnp.max(s, axis=0, keepdims=True), sink)
        return jnp.exp2(s - m).astype(BF16), jnp.exp2(sink - m)

    def output(ch, p, sink_p):
        half = ch // 2
        acc = _dot_tn(v[:, half * LANES:(half + 1) * LANES], p)
        o = acc[0:HEAD_DIM] / (acc[HEAD_DIM:HEAD_DIM + 1] + sink_p)
        return [o[:, 0:C_TQ], o[:, C_TQ:]]

    heads = []
    ss, ps = {}, {}
    for ch in range(n_chunk + 2):
        if ch < n_chunk:
            ss[ch] = scores(ch)
        if 1 <= ch <= n_chunk:
            ps[ch - 1] = probs(ch - 1, ss.pop(ch - 1))
        if ch >= 2:
            heads += output(ch - 2, *ps.pop(ch - 2))
    for j in range(grp):
        x = jnp.concatenate([heads[j], heads[grp + j]], axis=0)
        o_ref[:, j * LANES:(j + 1) * LANES] = x.T.astype(o_ref.dtype)


def _attn_c(cq, ck, cv, tab, sink_row):
    bsz, t, _ = cq.shape
    n_pair = C_KV_HEADS // 2
    cols = (C_Q_HEADS // n_pair) * C_TQ
    qspec = pl.BlockSpec((None, C_TQ, CQ_W // n_pair), lambda b, p, i: (b, i, p))
    kvspec = pl.BlockSpec((None, t + 2 * PROJ_TM, PAIR_W), lambda b, p, i: (b, 0, p))
    return pl.pallas_call(
        _attn_c_body,
        out_shape=jax.ShapeDtypeStruct(cq.shape, BF16),
        grid=(bsz, n_pair, t // C_TQ),
        in_specs=[qspec, kvspec, kvspec,
                  pl.BlockSpec((None, C_KW, cols), lambda b, p, i: (p, 0, 0)),
                  pl.BlockSpec((None, 1, cols), lambda b, p, i: (p, 0, 0))],
        out_specs=qspec,
        compiler_params=_cparams(("parallel", "parallel", "arbitrary")),
        name="attn_banded",
    )(cq, ck, cv, tab, sink_row)


def _alibi_slopes(n):
    return (2.0 ** (-8.0 * (np.arange(n) + 1) / n)).astype(np.float32).astype(np.float64)


def _dilated_tables():
    slopes = _alibi_slopes(A_HEADS)[:, None, None]

    def table(off):
        mult = np.zeros(off.shape)
        for w, d in A_BRANCHES:
            mult += (off % d == 0) & (np.abs(off) <= w // 2)
        bias = (-slopes * np.abs(off)[None] + np.log(np.maximum(mult, 1.0))[None]) * LOG2E
        tab = np.where(mult[None] > 0, bias, NEG).astype(np.float32)
        tab = tab.reshape(A_HEADS // 2, 2, off.shape[0], A_TQ).transpose(0, 2, 1, 3)
        return tab.reshape(A_HEADS // 2, off.shape[0], 2 * A_TQ)

    qi = np.arange(A_TQ)[None, :]
    same = table(N_CLASS * (np.arange(A_SAME_W)[:, None] - 256 - qi))
    oth = []
    for delta in range(-(N_CLASS - 1), N_CLASS):
        off = N_CLASS * (np.arange(A_OTH_W)[:, None] - A_OTH_OFF - qi) + delta
        oth.append(table(off) if delta != 0 else np.full((A_HEADS // 2, A_OTH_W, 2 * A_TQ), NEG, np.float32))
    return jnp.asarray(same), jnp.asarray(np.stack(oth, axis=1))


def _c_head(p, half, j):
    grp = C_Q_HEADS // C_KV_HEADS
    return 2 * grp * p + grp * half + j


def _c_head_order():
    grp = C_Q_HEADS // C_KV_HEADS
    return [_c_head(p, half, j) for p in range(C_KV_HEADS // 2) for j in range(grp) for half in range(2)]


def _banded_table():
    slopes = _alibi_slopes(C_Q_HEADS)
    grp = C_Q_HEADS // C_KV_HEADS
    rel = (np.arange(C_KW)[:, None] - C_TQ) - np.arange(C_TQ)[None, :]
    tab = np.zeros((C_KV_HEADS // 2, C_KW, 2 * grp, C_TQ), np.float32)
    for p in range(C_KV_HEADS // 2):
        for half in range(2):
            for j in range(grp):
                bias = -slopes[_c_head(p, half, j)] * np.abs(rel) * LOG2E
                tab[p, :, half * grp + j, :] = np.where(np.abs(rel) <= C_WINDOW, bias, NEG)
    return jnp.asarray(tab.reshape(C_KV_HEADS // 2, C_KW, 2 * grp * C_TQ))


def _rope_tables(t):
    n_rows = t // GRID_W
    half = HEAD_DIM // 2
    row = jnp.repeat(jnp.arange(n_rows), GRID_W).astype(F32)
    col = jnp.tile(jnp.arange(GRID_W), n_rows).astype(F32)
    fr = ROPE_THETA ** (-jnp.arange(0, half, 2, dtype=F32) / half)
    ang = jnp.concatenate([row[:, None] * fr, col[:, None] * fr], axis=-1)
    cos, sin = jnp.cos(ang), jnp.sin(ang)
    cos_h = jnp.concatenate([cos, cos], axis=-1)
    sin_h = jnp.concatenate([-sin, sin], axis=-1)

    def cm(x):
        x = jnp.concatenate([x, x], axis=-1)
        return x.reshape(t // N_CLASS, N_CLASS, LANES).transpose(1, 0, 2)

    return cm(cos_h), cm(sin_h)


_DEINT = np.concatenate([np.arange(0, HEAD_DIM, 2), np.arange(1, HEAD_DIM, 2)])
_B_HEAD_ORDER = [0, 4, 1, 5, 2, 6, 3, 7]


def _head_cols(heads, base, dims):
    return np.concatenate([base + h * HEAD_DIM + dims for h in heads])


def _prep_ab_weights(w_in, w_out, q_gain, k_gain):
    nat = np.arange(HEAD_DIM)
    o = 3 * A_W
    cols = np.concatenate([
        np.arange(o),
        _head_cols(_B_HEAD_ORDER, o, _DEINT),
        _head_cols(range(B_KV_HEADS), o + BQ_W, _DEINT),
        np.arange(o + BQ_W + BKV_W, o + BQ_W + 2 * BKV_W)])
    w = w_in[:, cols].astype(BF16)
    wa = w_out[:A_W].astype(BF16)
    wb = w_out[_head_cols(_B_HEAD_ORDER, A_W, nat)].astype(BF16)
    qg = jnp.tile(q_gain[_DEINT], B_Q_HEADS)[None, :]
    kg = jnp.tile(k_gain[_DEINT], B_KV_HEADS)[None, :]
    return w, wa, wb, qg, kg


def _prep_c_weights(w_in, w_out, sink):
    nat = np.arange(HEAD_DIM)
    qcols = _head_cols(_c_head_order(), 0, nat)
    cols = np.concatenate([qcols, np.arange(CQ_W, CQ_W + 2 * CKV_W)])
    w = w_in[:, cols].astype(BF16)
    wo = w_out[qcols].astype(BF16)
    grp = C_Q_HEADS // C_KV_HEADS
    idx = np.array([[_c_head(p, half, j) for half in range(2) for j in range(grp)]
                    for p in range(C_KV_HEADS // 2)])
    sink_row = jnp.repeat(sink[idx] * LOG2E, C_TQ, axis=-1)[:, None, :].astype(F32)
    return w, wo, sink_row


def _block_ones(width):
    g = np.arange(width) // HEAD_DIM
    return jnp.asarray((g[:, None] == g[None, :]).astype(np.float32)).astype(BF16)


def _trunk(x, p):
    bsz, t, _ = x.shape
    tc = t // N_CLASS
    cos_t, sin_t = _rope_tables(t)

    h = _ffn(x, [], p["g"][0][0], *p["ffn"][0][0], p["fg"], cm_out=True)
    aq, ak, av, bq, bk, bv = _ab_in(h, p["g"][0][1], p["ab_w"], p["qg"], p["kg"],
                                    cos_t, sin_t, p["gq"], p["gk"])
    a_out = _attn_a(aq, ak, av, p["tab_same"], p["tab_oth"])
    b_out = _attn_b(bq.reshape(bsz, t, BQ_W), bk.reshape(bsz, t, BKV_W), bv.reshape(bsz, t, PAIR_W))
    mix = [(a_out, p["ab_wa"]), (b_out.reshape(bsz, N_CLASS, tc, BQ_W), p["ab_wb"])]
    h = _ffn(h, mix, p["g"][0][2], *p["ffn"][0][1], p["fg"], cm_in=True)

    h = _ffn(h, [], p["g"][1][0], *p["ffn"][1][0], p["fg"])
    cq, ck, cv = _c_in(h, p["g"][1][1], p["c_w"])
    c_out = _attn_c(cq, ck, cv, p["tab_c"], p["sink"])
    return _ffn(h, [(c_out, p["c_wo"])], p["g"][1][2], *p["ffn"][1][1], p["fg"], final=True)


def kernel(x_prompt, x_sample, norm_g, ffn_w1, ffn_w3, ffn_w2, ab_w_in, ab_w_out, ab_q_gain, ab_k_gain,
           c_w_in, c_w_out, c_sink, final_g):
    tab_same, tab_oth = _dilated_tables()
    ab_w, ab_wa, ab_wb, qg, kg = _prep_ab_weights(ab_w_in[0], ab_w_out[0], ab_q_gain[0], ab_k_gain[0])
    c_w, c_wo, sink = _prep_c_weights(c_w_in[0], c_w_out[0], c_sink[0])
    p = {
        "g": [[norm_g[l, k][None, :] for k in range(3)] for l in range(2)],
        "ffn": [[(ffn_w1[l, k].astype(BF16), ffn_w3[l, k].astype(BF16), ffn_w2[l, k].astype(BF16))
                 for k in range(2)] for l in range(2)],
        "fg": final_g[None, :],
        "ab_w": ab_w, "ab_wa": ab_wa, "ab_wb": ab_wb, "qg": qg, "kg": kg,
        "gq": _block_ones(BQ_W), "gk": _block_ones(BKV_W),
        "tab_same": tab_same, "tab_oth": tab_oth,
        "c_w": c_w, "c_wo": c_wo, "sink": sink, "tab_c": _banded_table(),
    }
    return _trunk(x_prompt, p), _trunk(x_sample, p)
```

```python
import functools

import numpy as np
import jax
import jax.numpy as jnp
from jax import lax
from jax.experimental import pallas as pl
from jax.experimental.pallas import tpu as pltpu

F32 = jnp.float32
BF16 = jnp.bfloat16

D_MODEL = 1024
HEAD_DIM = 64
D_FF = 2816
EPS = 1e-6
NEG = -1e30
GRID_W = 64
ROPE_THETA = 10000.0
A_HEADS = 8
A_BRANCHES = ((128, 1), (512, 4), (2048, 16))
B_Q_HEADS = 8
B_KV_HEADS = 2
C_Q_HEADS = 16
C_KV_HEADS = 4
C_WINDOW = 128
A_W = A_HEADS * HEAD_DIM
BQ_W = B_Q_HEADS * HEAD_DIM
BKV_W = B_KV_HEADS * HEAD_DIM
CQ_W = C_Q_HEADS * HEAD_DIM
CKV_W = C_KV_HEADS * HEAD_DIM
LOG2E = 1.4426950408889634
QSCALE = HEAD_DIM ** -0.5 * LOG2E

LANES = 128
PAIR_W = 2 * LANES
N_CLASS = 4
VMEM_LIMIT = 56 * 1024 * 1024

FFN_TM = 512
FF_CHUNK = 256
PROJ_TM = 512
A_TQ = 128
A_NQ = 2
A_LAG_SOFTMAX = 2
A_LAG_PV = 4
A_SAME_W = 640
A_OTH_W = 192
A_OTH_OFF = 32
B_TQ = 256
B_TK = 512
B_LAG_PV = 1
C_TQ = 128
C_NQ = 8
C_LAG_SOFTMAX = 2
C_LAG_PV = 4
C_KW = 3 * C_TQ


def _cparams(sem):
    return pltpu.CompilerParams(dimension_semantics=sem, vmem_limit_bytes=VMEM_LIMIT)


def _rms(x, g):
    return x * lax.rsqrt(jnp.mean(x * x, axis=-1, keepdims=True) + EPS) * g


def _dot(a, b):
    return jnp.dot(a, b, preferred_element_type=F32)


def _dot_nt(a, b):
    return lax.dot_general(a, b, (((1,), (1,)), ((), ())), preferred_element_type=F32)


def _dot_tn(a, b):
    return lax.dot_general(a, b, (((0,), (0,)), ((), ())), preferred_element_type=F32)


def _resident(shape):
    nd = len(shape)
    return pl.BlockSpec(shape, lambda *_: (0,) * nd, pipeline_mode=pl.Buffered(1))


def _lane_iota(rows):
    return lax.broadcasted_iota(jnp.int32, (rows, LANES), 1)


def _ffn_body(*refs, final, cm_in, cm_out, n_mix):
    h_ref = refs[0]
    mix_refs = refs[1:1 + n_mix]
    mixw_refs = refs[1 + n_mix:1 + 2 * n_mix]
    g_ref, w1_ref, w3_ref, w2_ref, fg_ref, o_ref = refs[1 + 2 * n_mix:7 + 2 * n_mix]
    scratch = refs[7 + 2 * n_mix:]
    sub = FFN_TM // N_CLASS
    n_col = D_MODEL // LANES
    x = h_ref[...].reshape(FFN_TM, D_MODEL) if cm_in else h_ref[...]
    for a_ref, w_ref in zip(mix_refs, mixw_refs):
        a = a_ref[...]
        x = x + _dot(a.reshape(FFN_TM, a.shape[-1]) if cm_in else a, w_ref[...])
    n = _rms(x, g_ref[...]).astype(BF16)
    acc = jnp.zeros(x.shape, F32)
    for c in range(D_FF // FF_CHUNK):
        sl = slice(c * FF_CHUNK, (c + 1) * FF_CHUNK)
        a = _dot(n, w1_ref[:, sl])
        b = _dot(n, w3_ref[:, sl])
        hm = (a * jax.nn.sigmoid(a) * b).astype(BF16)
        acc = acc + _dot(hm, w2_ref[sl, :])
    y = x + 0.5 * acc
    if final:
        y = _rms(y, fg_ref[...])
    if cm_out:
        perm_ref, = scratch
        for j in range(n_col):
            perm_ref[j] = y[:, j * LANES:(j + 1) * LANES]
        for c in range(N_CLASS):
            for j in range(n_col):
                o_ref[c, :, j * LANES:(j + 1) * LANES] = perm_ref[j, pl.ds(c, sub, stride=N_CLASS), :]
    elif cm_in:
        perm_ref, = scratch
        for c in range(N_CLASS):
            for j in range(n_col):
                perm_ref[j, pl.ds(c, sub, stride=N_CLASS), :] = y[c * sub:(c + 1) * sub, j * LANES:(j + 1) * LANES]
        o_ref[...] = jnp.concatenate([perm_ref[j] for j in range(n_col)], axis=1)
    else:
        o_ref[...] = y


def _ffn(h, mix, g, w1, w3, w2, fg, *, final=False, cm_in=False, cm_out=False):
    bsz = h.shape[0]
    t = h.shape[1] * h.shape[2] if cm_in else h.shape[1]
    tc = t // N_CLASS
    sub = FFN_TM // N_CLASS

    def rows(width, class_major):
        if class_major:
            return pl.BlockSpec((None, N_CLASS, sub, width), lambda b, i: (b, 0, i, 0))
        return pl.BlockSpec((None, FFN_TM, width), lambda b, i: (b, i, 0))

    acts = [a for a, _ in mix]
    ws = [w for _, w in mix]
    out_shape = (bsz, N_CLASS, tc, D_MODEL) if cm_out else (bsz, t, D_MODEL)
    return pl.pallas_call(
        functools.partial(_ffn_body, final=final, cm_in=cm_in, cm_out=cm_out, n_mix=len(mix)),
        out_shape=jax.ShapeDtypeStruct(out_shape, F32),
        grid=(bsz, t // FFN_TM),
        in_specs=([rows(D_MODEL, cm_in)] + [rows(a.shape[-1], cm_in) for a in acts]
                  + [_resident(w.shape) for w in ws]
                  + [_resident((1, D_MODEL)), _resident(w1.shape), _resident(w3.shape),
                     _resident(w2.shape), _resident((1, D_MODEL))]),
        out_specs=rows(D_MODEL, cm_out),
        scratch_shapes=[pltpu.VMEM((D_MODEL // LANES, FFN_TM, LANES), F32)] if (cm_in or cm_out) else [],
        compiler_params=_cparams(("parallel", "parallel")),
        name="ffn_final" if final else "ffn",
    )(h, *acts, *ws, g, w1, w3, w2, fg)


def _store_k_aug(k_ref, kf, n_pair):
    zeros = jnp.zeros((kf.shape[0], LANES), BF16)
    for p in range(n_pair):
        k_ref[:, p * PAIR_W:p * PAIR_W + LANES] = kf[:, p * LANES:(p + 1) * LANES].astype(BF16)
        k_ref[:, p * PAIR_W + LANES:(p + 1) * PAIR_W] = zeros


def _store_k_halo(k_ref):
    lane = lax.broadcasted_iota(jnp.int32, k_ref.shape, 1)
    k_ref[...] = jnp.where(lane % PAIR_W == LANES, NEG, 0.0).astype(BF16)


def _store_v_aug(v_ref, vf, n_pair):
    lane = _lane_iota(vf.shape[0])
    lower = lane < HEAD_DIM
    one = jnp.where(lane == HEAD_DIM, 1.0, 0.0).astype(F32)
    for p in range(n_pair):
        vp = vf[:, p * LANES:(p + 1) * LANES]
        v_ref[:, p * PAIR_W:p * PAIR_W + LANES] = jnp.where(lower, vp, one).astype(BF16)
        v_ref[:, p * PAIR_W + LANES:(p + 1) * PAIR_W] = (
            jnp.where(lower, pltpu.roll(vp, HEAD_DIM, 1), one).astype(BF16))


def _group_mean_sq(x, gmat):
    ss = x * x
    hi = ss.astype(BF16)
    lo = (ss - hi.astype(F32)).astype(BF16)
    return (_dot(hi, gmat) + _dot(lo, gmat)) * (1.0 / HEAD_DIM)


def _rope(y, cos, sin_signed):
    first_half = (_lane_iota(y.shape[0]) & (HEAD_DIM // 2)) == 0
    outs = []
    for j in range(y.shape[1] // LANES):
        yb = y[:, j * LANES:(j + 1) * LANES]
        partner = jnp.where(first_half,
                            pltpu.roll(yb, LANES - HEAD_DIM // 2, 1),
                            pltpu.roll(yb, HEAD_DIM // 2, 1))
        outs.append(yb * cos + partner * sin_signed)
    return outs[0] if len(outs) == 1 else jnp.concatenate(outs, axis=1)


def _ab_in_body(h_ref, g_ref, w_ref, qg_ref, kg_ref, cos_ref, sin_ref, gq_ref, gk_ref,
                aq_ref, ak_ref, av_ref, bq_ref, bk_ref, bv_ref, *, nblk):
    i = pl.program_id(2)
    is_halo = jnp.logical_or(i == 0, i == nblk + 1)

    @pl.when(is_halo)
    def _():
        _store_k_halo(ak_ref)
        av_ref[...] = jnp.zeros(av_ref.shape, BF16)

    @pl.when(jnp.logical_not(is_halo))
    def _():
        u = _rms(h_ref[...], g_ref[...]).astype(BF16)
        o = 3 * A_W
        pb = _dot(u, w_ref[:, o:])
        pa = _dot(u, w_ref[:, 0:o])
        cos = cos_ref[...]
        sin = sin_ref[...]
        xq = pb[:, 0:BQ_W]
        yq = xq * lax.rsqrt(_group_mean_sq(xq, gq_ref[...]) + EPS) * qg_ref[...]
        bq_ref[...] = (_rope(yq, cos, sin) * QSCALE).astype(BF16)
        xk = pb[:, BQ_W:BQ_W + BKV_W]
        yk = xk * lax.rsqrt(_group_mean_sq(xk, gk_ref[...]) + EPS) * kg_ref[...]
        bk_ref[...] = _rope(yk, cos, sin).astype(BF16)
        _store_v_aug(bv_ref, pb[:, BQ_W + BKV_W:BQ_W + 2 * BKV_W], B_KV_HEADS // 2)
        aq_ref[...] = (pa[:, 0:A_W] * QSCALE).astype(BF16)
        _store_k_aug(ak_ref, pa[:, A_W:2 * A_W], A_HEADS // 2)
        _store_v_aug(av_ref, pa[:, 2 * A_W:3 * A_W], A_HEADS // 2)


def _ab_in(h, g, w, qg, kg, cos_t, sin_t, gq, gk):
    bsz, _, tc, _ = h.shape
    nblk = tc // PROJ_TM

    def inner(i):
        return jnp.clip(i - 1, 0, nblk - 1)

    def cm(width):
        return pl.BlockSpec((None, None, PROJ_TM, width), lambda b, c, i: (b, c, inner(i), 0))

    def cm_halo(width):
        return pl.BlockSpec((None, None, PROJ_TM, width), lambda b, c, i: (b, c, i, 0))

    tab = pl.BlockSpec((None, PROJ_TM, LANES), lambda b, c, i: (c, inner(i), 0))
    sds = jax.ShapeDtypeStruct
    a_aug = (A_HEADS // 2) * PAIR_W
    return pl.pallas_call(
        functools.partial(_ab_in_body, nblk=nblk),
        out_shape=(sds((bsz, N_CLASS, tc, A_W), BF16),
                   sds((bsz, N_CLASS, tc + 2 * PROJ_TM, a_aug), BF16),
                   sds((bsz, N_CLASS, tc + 2 * PROJ_TM, a_aug), BF16),
                   sds((bsz, N_CLASS, tc, BQ_W), BF16),
                   sds((bsz, N_CLASS, tc, BKV_W), BF16),
                   sds((bsz, N_CLASS, tc, PAIR_W), BF16)),
        grid=(bsz, N_CLASS, nblk + 2),
        in_specs=[cm(D_MODEL), _resident((1, D_MODEL)), _resident(w.shape), _resident((1, BQ_W)),
                  _resident((1, BKV_W)), tab, tab, _resident(gq.shape), _resident(gk.shape)],
        out_specs=(cm(A_W), cm_halo(a_aug), cm_halo(a_aug), cm(BQ_W), cm(BKV_W), cm(PAIR_W)),
        compiler_params=_cparams(("parallel", "parallel", "arbitrary")),
        name="ab_in_proj",
    )(h, g, w, qg, kg, cos_t, sin_t, gq, gk)


def _aug_q(q, keep):
    e0 = jnp.where(_lane_iota(q.shape[0]) == 0, 1.0, 0.0).astype(BF16)
    return jnp.concatenate([jnp.where(keep, q, jnp.zeros_like(q)), e0], axis=1)


def _attn_a_body(q_ref, k_ref, v_ref, ts_ref, to_ref, o_ref):
    base = pl.program_id(2) * (A_NQ * A_TQ)
    lower = _lane_iota(A_TQ) < HEAD_DIM
    upper = jnp.logical_not(lower)

    def windows(qb, c):
        u0 = base + qb * A_TQ
        s_start = pl.multiple_of(u0 + (PROJ_TM - 256), A_TQ)
        o_start = pl.multiple_of(u0 + (PROJ_TM - A_OTH_OFF), 32)
        wins = [(c, s_start, A_SAME_W, ts_ref[...])]
        for cp in range(N_CLASS):
            if cp != c:
                wins.append((cp, o_start, A_OTH_W, to_ref[cp - c + (N_CLASS - 1)]))
        return wins

    def scores(qb, c):
        q = q_ref[c, qb * A_TQ:(qb + 1) * A_TQ, :]
        qp = jnp.concatenate([_aug_q(q, lower), _aug_q(q, upper)], axis=0)
        return [_dot_nt(k_ref[cp, pl.ds(st, w), :], qp) + tab for cp, st, w, tab in windows(qb, c)]

    def probs(ss):
        m = jnp.max(ss[0], axis=0, keepdims=True)
        for s in ss[1:]:
            m = jnp.maximum(m, jnp.max(s, axis=0, keepdims=True))
        return [jnp.exp2(s - m).astype(BF16) for s in ss]

    def output(qb, c, ps):
        acc = None
        for (cp, st, w, _), p in zip(windows(qb, c), ps):
            d = _dot_tn(v_ref[cp, pl.ds(st, w), :], p)
            acc = d if acc is None else acc + d
        lo = acc[0:HEAD_DIM, 0:A_TQ] / acc[HEAD_DIM:HEAD_DIM + 1, 0:A_TQ]
        hi = acc[LANES:LANES + HEAD_DIM, A_TQ:] / acc[LANES + HEAD_DIM:LANES + HEAD_DIM + 1, A_TQ:]
        o_ref[c, qb * A_TQ:(qb + 1) * A_TQ, :] = jnp.concatenate([lo, hi], axis=0).T.astype(o_ref.dtype)

    items = [(qb, c) for qb in range(A_NQ) for c in range(N_CLASS)]
    ss, ps = {}, {}
    for i in range(len(items) + A_LAG_PV):
        if i < len(items):
            ss[i] = scores(*items[i])
        if A_LAG_SOFTMAX <= i < len(items) + A_LAG_SOFTMAX:
            ps[i - A_LAG_SOFTMAX] = probs(ss.pop(i - A_LAG_SOFTMAX))
        if i >= A_LAG_PV:
            output(*items[i - A_LAG_PV], ps.pop(i - A_LAG_PV))


def _attn_a(aq, ak, av, tab_same, tab_oth):
    bsz, _, tc, _ = aq.shape
    qspec = pl.BlockSpec((None, N_CLASS, A_NQ * A_TQ, LANES), lambda b, p, i: (b, 0, i, p))
    kvspec = pl.BlockSpec((None, N_CLASS, tc + 2 * PROJ_TM, PAIR_W), lambda b, p, i: (b, 0, 0, p),
                          pipeline_mode=pl.Buffered(1))
    return pl.pallas_call(
        _attn_a_body,
        out_shape=jax.ShapeDtypeStruct(aq.shape, BF16),
        grid=(bsz, A_HEADS // 2, tc // (A_NQ * A_TQ)),
        in_specs=[qspec, kvspec, kvspec,
                  pl.BlockSpec((None, A_SAME_W, 2 * A_TQ), lambda b, p, i: (p, 0, 0)),
                  pl.BlockSpec((None, 2 * N_CLASS - 1, A_OTH_W, 2 * A_TQ), lambda b, p, i: (p, 0, 0, 0))],
        out_specs=qspec,
        compiler_params=_cparams(("parallel", "parallel", "arbitrary")),
        name="attn_dilated",
    )(aq, ak, av, tab_same, tab_oth)


def _attn_b_body(q_ref, k_ref, v_ref, o_ref, qp_ref, m_ref, acc_ref, s_ref, *, t):
    grp = B_Q_HEADS // B_KV_HEADS
    lower = _lane_iota(B_TQ) < HEAD_DIM
    zero = jnp.zeros((B_TQ, LANES), BF16)
    for j in range(grp):
        qb = q_ref[:, j * LANES:(j + 1) * LANES]
        qp_ref[0, j * B_TQ:(j + 1) * B_TQ, :] = jnp.where(lower, qb, zero)
        qp_ref[1, j * B_TQ:(j + 1) * B_TQ, :] = jnp.where(lower, zero, qb)
    m_ref[...] = jnp.full(m_ref.shape, NEG, F32)
    acc_ref[...] = jnp.zeros(acc_ref.shape, F32)
    n_chunk = B_KV_HEADS * grp
    n_steps = t // B_TK

    def cols(c):
        g, j = divmod(c, grp)
        return g, slice(j * B_TQ, (j + 1) * B_TQ)

    def rows(kb):
        return pl.ds(pl.multiple_of(kb * B_TK, B_TK), B_TK)

    def qk(k, c, slot):
        g, cs = cols(c)
        s_ref[slot, c] = _dot_nt(k, qp_ref[g, cs, :])

    def softmax(c, slot):
        g, cs = cols(c)
        s = s_ref[slot, c]
        m_old = m_ref[g, :, cs]
        m_new = jnp.maximum(m_old, jnp.max(s, axis=0, keepdims=True))
        m_ref[g, :, cs] = m_new
        return jnp.exp2(s - m_new).astype(BF16), jnp.exp2(m_old - m_new)

    def pv(v, c, p, alpha):
        g, cs = cols(c)
        acc_ref[g, :, cs] = alpha * acc_ref[g, :, cs] + _dot_tn(v[:, g * LANES:(g + 1) * LANES], p)

    def block(kb, slot, with_next):
        v = v_ref[rows(kb), :]
        k_next = k_ref[rows(kb + 1), :] if with_next else None
        pp = {}
        for c in range(n_chunk + B_LAG_PV):
            if c < n_chunk:
                pp[c] = softmax(c, slot)
                if with_next:
                    qk(k_next, c, 1 - slot)
            if c >= B_LAG_PV:
                pv(v, c - B_LAG_PV, *pp.pop(c - B_LAG_PV))

    k0 = k_ref[rows(0), :]
    for c in range(n_chunk):
        qk(k0, c, 0)

    def pair(i, carry):
        block(2 * i, 0, True)
        block(2 * i + 1, 1, True)
        return carry

    lax.fori_loop(0, n_steps // 2 - 1, pair, 0)
    block(n_steps - 2, 0, True)
    block(n_steps - 1, 1, False)
    for j in range(grp):
        cs = slice(j * B_TQ, (j + 1) * B_TQ)
        x = jnp.concatenate([acc_ref[g, 0:HEAD_DIM, cs] / acc_ref[g, HEAD_DIM:HEAD_DIM + 1, cs]
                             for g in range(B_KV_HEADS)], axis=0)
        o_ref[:, j * LANES:(j + 1) * LANES] = x.T.astype(o_ref.dtype)


def _attn_b(bq, bk, bv):
    bsz, t, _ = bq.shape
    rows = (B_Q_HEADS // B_KV_HEADS) * B_TQ
    qspec = pl.BlockSpec((None, B_TQ, BQ_W), lambda b, i: (b, i, 0))
    return pl.pallas_call(
        functools.partial(_attn_b_body, t=t),
        out_shape=jax.ShapeDtypeStruct(bq.shape, BF16),
        grid=(bsz, t // B_TQ),
        in_specs=[qspec,
                  pl.BlockSpec((None, t, BKV_W), lambda b, i: (b, 0, 0)),
                  pl.BlockSpec((None, t, PAIR_W), lambda b, i: (b, 0, 0))],
        out_specs=qspec,
        scratch_shapes=[pltpu.VMEM((B_KV_HEADS, rows, LANES), BF16),
                        pltpu.VMEM((B_KV_HEADS, 1, rows), F32),
                        pltpu.VMEM((B_KV_HEADS, LANES, rows), F32),
                        pltpu.VMEM((2, B_Q_HEADS, B_TK, B_TQ), F32)],
        compiler_params=_cparams(("parallel", "arbitrary")),
        name="attn_global",
    )(bq, bk, bv)


def _c_in_body(h_ref, g_ref, w_ref, q_ref, k_ref, v_ref, *, nblk):
    i = pl.program_id(1)
    is_halo = jnp.logical_or(i == 0, i == nblk + 1)

    @pl.when(is_halo)
    def _():
        _store_k_halo(k_ref)
        v_ref[...] = jnp.zeros(v_ref.shape, BF16)

    @pl.when(jnp.logical_not(is_halo))
    def _():
        u = _rms(h_ref[...], g_ref[...]).astype(BF16)
        proj = _dot(u, w_ref[...])
        q_ref[...] = (proj[:, 0:CQ_W] * QSCALE).astype(BF16)
        _store_k_aug(k_ref, proj[:, CQ_W:CQ_W + CKV_W], C_KV_HEADS // 2)
        _store_v_aug(v_ref, proj[:, CQ_W + CKV_W:CQ_W + 2 * CKV_W], C_KV_HEADS // 2)


def _c_in(h, g, w):
    bsz, t, _ = h.shape
    nblk = t // PROJ_TM
    aug = (C_KV_HEADS // 2) * PAIR_W

    def inner(i):
        return jnp.clip(i - 1, 0, nblk - 1)

    sds = jax.ShapeDtypeStruct
    return pl.pallas_call(
        functools.partial(_c_in_body, nblk=nblk),
        out_shape=(sds((bsz, t, CQ_W), BF16),
                   sds((bsz, t + 2 * PROJ_TM, aug), BF16),
                   sds((bsz, t + 2 * PROJ_TM, aug), BF16)),
        grid=(bsz, nblk + 2),
        in_specs=[pl.BlockSpec((None, PROJ_TM, D_MODEL), lambda b, i: (b, inner(i), 0)),
                  _resident((1, D_MODEL)), _resident(w.shape)],
        out_specs=(pl.BlockSpec((None, PROJ_TM, CQ_W), lambda b, i: (b, inner(i), 0)),
                   pl.BlockSpec((None, PROJ_TM, aug), lambda b, i: (b, i, 0)),
                   pl.BlockSpec((None, PROJ_TM, aug), lambda b, i: (b, i, 0))),
        compiler_params=_cparams(("parallel", "arbitrary")),
        name="c_in_proj",
    )(h, g, w)


def _attn_c_body(q_ref, k_ref, v_ref, tab_ref, sink_ref, o_ref):
    base = pl.program_id(2) * (C_NQ * C_TQ)
    grp = C_Q_HEADS // C_KV_HEADS
    lower = _lane_iota(C_TQ) < HEAD_DIM
    n_chunk = grp

    def cols(ch):
        return slice(ch * 2 * C_TQ, (ch + 1) * 2 * C_TQ)

    def kv_rows(qb):
        return pl.ds(pl.multiple_of(base + qb * C_TQ + (PROJ_TM - C_TQ), C_TQ), C_KW)

    def scores(qb, ch):
        half, js = ch // 2, (2 * (ch % 2), 2 * (ch % 2) + 1)
        keep = lower if half == 0 else jnp.logical_not(lower)
        qs = slice(qb * C_TQ, (qb + 1) * C_TQ)
        qp = jnp.concatenate([_aug_q(q_ref[qs, j * LANES:(j + 1) * LANES], keep) for j in js], axis=0)
        return _dot_nt(k_ref[kv_rows(qb), :], qp) + tab_ref[:, cols(ch)]

    def probs(ch, s):
        sink = sink_ref[:, cols(ch)]
        m = jnp.maximum(jnp.max(s, axis=0, keepdims=True), sink)
        return jnp.exp2(s - m).astype(BF16), jnp.exp2(sink - m)

    def output(qb, ch, p, sink_p):
        half = ch // 2
        acc = _dot_tn(v_ref[kv_rows(qb), half * LANES:(half + 1) * LANES], p)
        o = acc[0:HEAD_DIM] / (acc[HEAD_DIM:HEAD_DIM + 1] + sink_p)
        return [o[:, 0:C_TQ], o[:, C_TQ:]]

    items = [(qb, ch) for qb in range(C_NQ) for ch in range(n_chunk)]
    heads = {qb: [] for qb in range(C_NQ)}
    ss, ps = {}, {}
    for i in range(len(items) + C_LAG_PV):
        if i < len(items):
            ss[i] = scores(*items[i])
        if C_LAG_SOFTMAX <= i < len(items) + C_LAG_SOFTMAX:
            ps[i - C_LAG_SOFTMAX] = probs(items[i - C_LAG_SOFTMAX][1], ss.pop(i - C_LAG_SOFTMAX))
        if i >= C_LAG_PV:
            qb, ch = items[i - C_LAG_PV]
            heads[qb] += output(qb, ch, *ps.pop(i - C_LAG_PV))
            if ch == n_chunk - 1:
                for j in range(grp):
                    x = jnp.concatenate([heads[qb][j], heads[qb][grp + j]], axis=0)
                    o_ref[qb * C_TQ:(qb + 1) * C_TQ, j * LANES:(j + 1) * LANES] = x.T.astype(o_ref.dtype)


def _attn_c(cq, ck, cv, tab, sink_row):
    bsz, t, _ = cq.shape
    n_pair = C_KV_HEADS // 2
    cols = (C_Q_HEADS // n_pair) * C_TQ
    qspec = pl.BlockSpec((None, C_NQ * C_TQ, CQ_W // n_pair), lambda b, p, i: (b, i, p))
    kvspec = pl.BlockSpec((None, t + 2 * PROJ_TM, PAIR_W), lambda b, p, i: (b, 0, p),
                          pipeline_mode=pl.Buffered(1))
    return pl.pallas_call(
        _attn_c_body,
        out_shape=jax.ShapeDtypeStruct(cq.shape, BF16),
        grid=(bsz, n_pair, t // (C_NQ * C_TQ)),
        in_specs=[qspec, kvspec, kvspec,
                  pl.BlockSpec((None, C_KW, cols), lambda b, p, i: (p, 0, 0)),
                  pl.BlockSpec((None, 1, cols), lambda b, p, i: (p, 0, 0))],
        out_specs=qspec,
        compiler_params=_cparams(("parallel", "parallel", "arbitrary")),
        name="attn_banded",
    )(cq, ck, cv, tab, sink_row)


def _alibi_slopes(n):
    return (2.0 ** (-8.0 * (np.arange(n) + 1) / n)).astype(np.float32).astype(np.float64)


def _dilated_tables():
    slopes = _alibi_slopes(A_HEADS)[:, None, None]

    def table(off):
        mult = np.zeros(off.shape)
        for w, d in A_BRANCHES:
            mult += (off % d == 0) & (np.abs(off) <= w // 2)
        bias = (-slopes * np.abs(off)[None] + np.log(np.maximum(mult, 1.0))[None]) * LOG2E
        tab = np.where(mult[None] > 0, bias, NEG).astype(np.float32)
        tab = tab.reshape(A_HEADS // 2, 2, off.shape[0], A_TQ).transpose(0, 2, 1, 3)
        return tab.reshape(A_HEADS // 2, off.shape[0], 2 * A_TQ)

    qi = np.arange(A_TQ)[None, :]
    same = table(N_CLASS * (np.arange(A_SAME_W)[:, None] - 256 - qi))
    oth = []
    for delta in range(-(N_CLASS - 1), N_CLASS):
        off = N_CLASS * (np.arange(A_OTH_W)[:, None] - A_OTH_OFF - qi) + delta
        oth.append(table(off) if delta != 0 else np.full((A_HEADS // 2, A_OTH_W, 2 * A_TQ), NEG, np.float32))
    return jnp.asarray(same), jnp.asarray(np.stack(oth, axis=1))


def _c_head(p, half, j):
    grp = C_Q_HEADS // C_KV_HEADS
    return 2 * grp * p + grp * half + j


def _c_head_order():
    grp = C_Q_HEADS // C_KV_HEADS
    return [_c_head(p, half, j) for p in range(C_KV_HEADS // 2) for j in range(grp) for half in range(2)]


def _banded_table():
    slopes = _alibi_slopes(C_Q_HEADS)
    grp = C_Q_HEADS // C_KV_HEADS
    rel = (np.arange(C_KW)[:, None] - C_TQ) - np.arange(C_TQ)[None, :]
    tab = np.zeros((C_KV_HEADS // 2, C_KW, 2 * grp, C_TQ), np.float32)
    for p in range(C_KV_HEADS // 2):
        for half in range(2):
            for j in range(grp):
                bias = -slopes[_c_head(p, half, j)] * np.abs(rel) * LOG2E
                tab[p, :, half * grp + j, :] = np.where(np.abs(rel) <= C_WINDOW, bias, NEG)
    return jnp.asarray(tab.reshape(C_KV_HEADS // 2, C_KW, 2 * grp * C_TQ))


def _rope_tables(t):
    n_rows = t // GRID_W
    half = HEAD_DIM // 2
    row = jnp.repeat(jnp.arange(n_rows), GRID_W).astype(F32)
    col = jnp.tile(jnp.arange(GRID_W), n_rows).astype(F32)
    fr = ROPE_THETA ** (-jnp.arange(0, half, 2, dtype=F32) / half)
    ang = jnp.concatenate([row[:, None] * fr, col[:, None] * fr], axis=-1)
    cos, sin = jnp.cos(ang), jnp.sin(ang)
    cos_h = jnp.concatenate([cos, cos], axis=-1)
    sin_h = jnp.concatenate([-sin, sin], axis=-1)

    def cm(x):
        x = jnp.concatenate([x, x], axis=-1)
        return x.reshape(t // N_CLASS, N_CLASS, LANES).transpose(1, 0, 2)

    return cm(cos_h), cm(sin_h)


_DEINT = np.concatenate([np.arange(0, HEAD_DIM, 2), np.arange(1, HEAD_DIM, 2)])
_B_HEAD_ORDER = [0, 4, 1, 5, 2, 6, 3, 7]


def _head_cols(heads, base, dims):
    return np.concatenate([base + h * HEAD_DIM + dims for h in heads])


def _prep_ab_weights(w_in, w_out, q_gain, k_gain):
    nat = np.arange(HEAD_DIM)
    o = 3 * A_W
    cols = np.concatenate([
        np.arange(o),
        _head_cols(_B_HEAD_ORDER, o, _DEINT),
        _head_cols(range(B_KV_HEADS), o + BQ_W, _DEINT),
        np.arange(o + BQ_W + BKV_W, o + BQ_W + 2 * BKV_W)])
    w = w_in[:, cols].astype(BF16)
    wa = w_out[:A_W].astype(BF16)
    wb = w_out[_head_cols(_B_HEAD_ORDER, A_W, nat)].astype(BF16)
    qg = jnp.tile(q_gain[_DEINT], B_Q_HEADS)[None, :]
    kg = jnp.tile(k_gain[_DEINT], B_KV_HEADS)[None, :]
    return w, wa, wb, qg, kg


def _prep_c_weights(w_in, w_out, sink):
    nat = np.arange(HEAD_DIM)
    qcols = _head_cols(_c_head_order(), 0, nat)
    cols = np.concatenate([qcols, np.arange(CQ_W, CQ_W + 2 * CKV_W)])
    w = w_in[:, cols].astype(BF16)
    wo = w_out[qcols].astype(BF16)
    grp = C_Q_HEADS // C_KV_HEADS
    idx = np.array([[_c_head(p, half, j) for half in range(2) for j in range(grp)]
                    for p in range(C_KV_HEADS // 2)])
    sink_row = jnp.repeat(sink[idx] * LOG2E, C_TQ, axis=-1)[:, None, :].astype(F32)
    return w, wo, sink_row


def _block_ones(width):
    g = np.arange(width) // HEAD_DIM
    return jnp.asarray((g[:, None] == g[None, :]).astype(np.float32)).astype(BF16)


def _trunk(x, p):
    bsz, t, _ = x.shape
    tc = t // N_CLASS
    cos_t, sin_t = _rope_tables(t)

    h = _ffn(x, [], p["g"][0][0], *p["ffn"][0][0], p["fg"], cm_out=True)
    aq, ak, av, bq, bk, bv = _ab_in(h, p["g"][0][1], p["ab_w"], p["qg"], p["kg"],
                                    cos_t, sin_t, p["gq"], p["gk"])
    a_out = _attn_a(aq, ak, av, p["tab_same"], p["tab_oth"])
    b_out = _attn_b(bq.reshape(bsz, t, BQ_W), bk.reshape(bsz, t, BKV_W), bv.reshape(bsz, t, PAIR_W))
    mix = [(a_out, p["ab_wa"]), (b_out.reshape(bsz, N_CLASS, tc, BQ_W), p["ab_wb"])]
    h = _ffn(h, mix, p["g"][0][2], *p["ffn"][0][1], p["fg"], cm_in=True)

    h = _ffn(h, [], p["g"][1][0], *p["ffn"][1][0], p["fg"])
    cq, ck, cv = _c_in(h, p["g"][1][1], p["c_w"])
    c_out = _attn_c(cq, ck, cv, p["tab_c"], p["sink"])
    return _ffn(h, [(c_out, p["c_wo"])], p["g"][1][2], *p["ffn"][1][1], p["fg"], final=True)


def kernel(x_prompt, x_sample, norm_g, ffn_w1, ffn_w3, ffn_w2, ab_w_in, ab_w_out, ab_q_gain, ab_k_gain,
           c_w_in, c_w_out, c_sink, final_g):
    tab_same, tab_oth = _dilated_tables()
    ab_w, ab_wa, ab_wb, qg, kg = _prep_ab_weights(ab_w_in[0], ab_w_out[0], ab_q_gain[0], ab_k_gain[0])
    c_w, c_wo, sink = _prep_c_weights(c_w_in[0], c_w_out[0], c_sink[0])
    p = {
        "g": [[norm_g[l, k][None, :] for k in range(3)] for l in range(2)],
        "ffn": [[(ffn_w1[l, k].astype(BF16), ffn_w3[l, k].astype(BF16), ffn_w2[l, k].astype(BF16))
                 for k in range(2)] for l in range(2)],
        "fg": final_g[None, :],
        "ab_w": ab_w, "ab_wa": ab_wa, "ab_wb": ab_wb, "qg": qg, "kg": kg,
        "gq": _block_ones(BQ_W), "gk": _block_ones(BKV_W),
        "tab_same": tab_same, "tab_oth": tab_oth,
        "c_w": c_w, "c_wo": c_wo, "sink": sink, "tab_c": _banded_table(),
    }
    return _trunk(x_prompt, p), _trunk(x_sample, p)
```

```python
import functools

import numpy as np
import jax
import jax.numpy as jnp
from jax import lax
from jax.experimental import pallas as pl
from jax.experimental.pallas import tpu as pltpu

F32 = jnp.float32
BF16 = jnp.bfloat16

D_MODEL = 1024
HEAD_DIM = 64
D_FF = 2816
EPS = 1e-6
NEG = -1e30
GRID_W = 64
ROPE_THETA = 10000.0
A_HEADS = 8
A_BRANCHES = ((128, 1), (512, 4), (2048, 16))
B_Q_HEADS = 8
B_KV_HEADS = 2
C_Q_HEADS = 16
C_KV_HEADS = 4
C_WINDOW = 128
A_W = A_HEADS * HEAD_DIM
BQ_W = B_Q_HEADS * HEAD_DIM
BKV_W = B_KV_HEADS * HEAD_DIM
CQ_W = C_Q_HEADS * HEAD_DIM
CKV_W = C_KV_HEADS * HEAD_DIM
LOG2E = 1.4426950408889634
QSCALE = HEAD_DIM ** -0.5 * LOG2E

LANES = 128
PAIR_W = 2 * LANES
N_CLASS = 4
VMEM_LIMIT = 56 * 1024 * 1024

FFN_TM = 512
FF_CHUNK = 256
PROJ_TM = 512
A_TQ = 128
A_NQ = 4
A_V_ROWS = 80
A_LAG_SOFTMAX = 2
A_LAG_PV = 4
A_SAME_W = 640
A_OTH_W = 192
A_OTH_OFF = 32
B_TQ = 256
B_TK = 512
B_UNROLL = 4
B_LAG_PV = 1
KV_DOUBLE_BUFFER_BYTES = 32 * 1024 * 1024
C_TQ = 128
C_NQ = 8
C_LAG_SOFTMAX = 2
C_LAG_PV = 4
C_KW = 3 * C_TQ


def _cparams(sem):
    return pltpu.CompilerParams(dimension_semantics=sem, vmem_limit_bytes=VMEM_LIMIT)


def _rms(x, g):
    return x * lax.rsqrt(jnp.mean(x * x, axis=-1, keepdims=True) + EPS) * g


def _dot(a, b):
    return jnp.dot(a, b, preferred_element_type=F32)


def _dot_nt(a, b):
    return lax.dot_general(a, b, (((1,), (1,)), ((), ())), preferred_element_type=F32)


def _dot_tn(a, b):
    return lax.dot_general(a, b, (((0,), (0,)), ((), ())), preferred_element_type=F32)


def _resident(shape):
    nd = len(shape)
    return pl.BlockSpec(shape, lambda *_: (0,) * nd, pipeline_mode=pl.Buffered(1))


def _kv_spec(block_shape, index_map):
    nbytes = 2 * 2 * int(np.prod([d for d in block_shape if d is not None]))
    if 2 * nbytes <= KV_DOUBLE_BUFFER_BYTES:
        return pl.BlockSpec(block_shape, index_map)
    return pl.BlockSpec(block_shape, index_map, pipeline_mode=pl.Buffered(1))


def _lane_iota(rows):
    return lax.broadcasted_iota(jnp.int32, (rows, LANES), 1)


def _ffn_body(*refs, final, cm_in, cm_out, n_mix):
    h_ref = refs[0]
    mix_refs = refs[1:1 + n_mix]
    mixw_refs = refs[1 + n_mix:1 + 2 * n_mix]
    g_ref, w1_ref, w3_ref, w2_ref, fg_ref, o_ref = refs[1 + 2 * n_mix:7 + 2 * n_mix]
    scratch = refs[7 + 2 * n_mix:]
    sub = FFN_TM // N_CLASS
    n_col = D_MODEL // LANES
    x = h_ref[...].reshape(FFN_TM, D_MODEL) if cm_in else h_ref[...]
    for a_ref, w_ref in zip(mix_refs, mixw_refs):
        a = a_ref[...]
        x = x + _dot(a.reshape(FFN_TM, a.shape[-1]) if cm_in else a, w_ref[...])
    n = _rms(x, g_ref[...]).astype(BF16)
    acc = jnp.zeros(x.shape, F32)
    for c in range(D_FF // FF_CHUNK):
        sl = slice(c * FF_CHUNK, (c + 1) * FF_CHUNK)
        a = _dot(n, w1_ref[:, sl])
        b = _dot(n, w3_ref[:, sl])
        hm = (a * jax.nn.sigmoid(a) * b).astype(BF16)
        acc = acc + _dot(hm, w2_ref[sl, :])
    y = x + 0.5 * acc
    if final:
        y = _rms(y, fg_ref[...])
    if cm_out:
        perm_ref, = scratch
        for j in range(n_col):
            perm_ref[j] = y[:, j * LANES:(j + 1) * LANES]
        for c in range(N_CLASS):
            for j in range(n_col):
                o_ref[c, :, j * LANES:(j + 1) * LANES] = perm_ref[j, pl.ds(c, sub, stride=N_CLASS), :]
    elif cm_in:
        perm_ref, = scratch
        for c in range(N_CLASS):
            for j in range(n_col):
                perm_ref[j, pl.ds(c, sub, stride=N_CLASS), :] = y[c * sub:(c + 1) * sub, j * LANES:(j + 1) * LANES]
        o_ref[...] = jnp.concatenate([perm_ref[j] for j in range(n_col)], axis=1)
    else:
        o_ref[...] = y


def _ffn(h, mix, g, ffn_w, idx, fg, *, final=False, cm_in=False, cm_out=False):
    bsz = h.shape[0]
    t = h.shape[1] * h.shape[2] if cm_in else h.shape[1]
    tc = t // N_CLASS
    sub = FFN_TM // N_CLASS

    def rows(width, class_major):
        if class_major:
            return pl.BlockSpec((None, N_CLASS, sub, width), lambda b, i: (b, 0, i, 0))
        return pl.BlockSpec((None, FFN_TM, width), lambda b, i: (b, i, 0))

    def weight(w):
        return pl.BlockSpec((None,) + w.shape[1:], lambda b, i: (idx, 0, 0), pipeline_mode=pl.Buffered(1))

    acts = [a for a, _ in mix]
    ws = [w for _, w in mix]
    out_shape = (bsz, N_CLASS, tc, D_MODEL) if cm_out else (bsz, t, D_MODEL)
    return pl.pallas_call(
        functools.partial(_ffn_body, final=final, cm_in=cm_in, cm_out=cm_out, n_mix=len(mix)),
        out_shape=jax.ShapeDtypeStruct(out_shape, F32),
        grid=(bsz, t // FFN_TM),
        in_specs=([rows(D_MODEL, cm_in)] + [rows(a.shape[-1], cm_in) for a in acts]
                  + [_resident(w.shape) for w in ws]
                  + [_resident((1, D_MODEL))] + [weight(w) for w in ffn_w] + [_resident((1, D_MODEL))]),
        out_specs=rows(D_MODEL, cm_out),
        scratch_shapes=[pltpu.VMEM((D_MODEL // LANES, FFN_TM, LANES), F32)] if (cm_in or cm_out) else [],
        compiler_params=_cparams(("parallel", "parallel")),
        name="ffn_final" if final else "ffn",
    )(h, *acts, *ws, g, *ffn_w, fg)


def _store_k_aug(k_ref, kf, n_pair):
    zeros = jnp.zeros((kf.shape[0], LANES), BF16)
    for p in range(n_pair):
        k_ref[:, p * PAIR_W:p * PAIR_W + LANES] = kf[:, p * LANES:(p + 1) * LANES].astype(BF16)
        k_ref[:, p * PAIR_W + LANES:(p + 1) * PAIR_W] = zeros


def _store_k_halo(k_ref):
    lane = lax.broadcasted_iota(jnp.int32, k_ref.shape, 1)
    k_ref[...] = jnp.where(lane % PAIR_W == LANES, NEG, 0.0).astype(BF16)


def _store_v_aug(v_ref, vf, n_pair):
    lane = _lane_iota(vf.shape[0])
    lower = lane < HEAD_DIM
    one = jnp.where(lane == HEAD_DIM, 1.0, 0.0).astype(F32)
    for p in range(n_pair):
        vp = vf[:, p * LANES:(p + 1) * LANES]
        v_ref[:, p * PAIR_W:p * PAIR_W + LANES] = jnp.where(lower, vp, one).astype(BF16)
        v_ref[:, p * PAIR_W + LANES:(p + 1) * PAIR_W] = (
            jnp.where(lower, pltpu.roll(vp, HEAD_DIM, 1), one).astype(BF16))


def _store_v_packed(v_ref, vf, n_pair):
    lane = _lane_iota(vf.shape[0])
    pad = A_V_ROWS - HEAD_DIM
    for p in range(n_pair):
        vp = vf[:, p * LANES:(p + 1) * LANES]
        shifted = pltpu.roll(vp, pad, 1)
        first = jnp.where(lane < HEAD_DIM, vp,
                          jnp.where(lane == HEAD_DIM, 1.0, jnp.where(lane >= A_V_ROWS, shifted, 0.0)))
        second = jnp.where(lane < pad, shifted, jnp.where(lane == pad, 1.0, 0.0))
        v_ref[:, p * PAIR_W:p * PAIR_W + LANES] = first.astype(BF16)
        v_ref[:, p * PAIR_W + LANES:(p + 1) * PAIR_W] = second.astype(BF16)


def _group_mean_sq(x, gmat):
    ss = x * x
    hi = ss.astype(BF16)
    lo = (ss - hi.astype(F32)).astype(BF16)
    return (_dot(hi, gmat) + _dot(lo, gmat)) * (1.0 / HEAD_DIM)


def _rope(y, cos, sin_signed):
    first_half = (_lane_iota(y.shape[0]) & (HEAD_DIM // 2)) == 0
    outs = []
    for j in range(y.shape[1] // LANES):
        yb = y[:, j * LANES:(j + 1) * LANES]
        partner = jnp.where(first_half,
                            pltpu.roll(yb, LANES - HEAD_DIM // 2, 1),
                            pltpu.roll(yb, HEAD_DIM // 2, 1))
        outs.append(yb * cos + partner * sin_signed)
    return outs[0] if len(outs) == 1 else jnp.concatenate(outs, axis=1)


def _ab_in_body(h_ref, g_ref, w_ref, qg_ref, kg_ref, cos_ref, sin_ref, gq_ref, gk_ref,
                aq_ref, ak_ref, av_ref, bq_ref, bk_ref, bv_ref, *, nblk):
    i = pl.program_id(2)
    is_halo = jnp.logical_or(i == 0, i == nblk + 1)

    @pl.when(is_halo)
    def _():
        _store_k_halo(ak_ref)
        av_ref[...] = jnp.zeros(av_ref.shape, BF16)

    @pl.when(jnp.logical_not(is_halo))
    def _():
        u = _rms(h_ref[...], g_ref[...]).astype(BF16)
        o = 3 * A_W
        pb = _dot(u, w_ref[:, o:])
        pa = _dot(u, w_ref[:, 0:o])
        cos = cos_ref[...]
        sin = sin_ref[...]
        xq = pb[:, 0:BQ_W]
        yq = xq * lax.rsqrt(_group_mean_sq(xq, gq_ref[...]) + EPS) * qg_ref[...]
        bq_ref[...] = (_rope(yq, cos, sin) * QSCALE).astype(BF16)
        xk = pb[:, BQ_W:BQ_W + BKV_W]
        yk = xk * lax.rsqrt(_group_mean_sq(xk, gk_ref[...]) + EPS) * kg_ref[...]
        bk_ref[...] = _rope(yk, cos, sin).astype(BF16)
        _store_v_aug(bv_ref, pb[:, BQ_W + BKV_W:BQ_W + 2 * BKV_W], B_KV_HEADS // 2)
        aq_ref[...] = (pa[:, 0:A_W] * QSCALE).astype(BF16)
        _store_k_aug(ak_ref, pa[:, A_W:2 * A_W], A_HEADS // 2)
        _store_v_packed(av_ref, pa[:, 2 * A_W:3 * A_W], A_HEADS // 2)


def _ab_in(h, g, w, qg, kg, cos_t, sin_t, gq, gk):
    bsz, _, tc, _ = h.shape
    nblk = tc // PROJ_TM

    def inner(i):
        return jnp.clip(i - 1, 0, nblk - 1)

    def cm(width):
        return pl.BlockSpec((None, None, PROJ_TM, width), lambda b, c, i: (b, c, inner(i), 0))

    def cm_halo(width):
        return pl.BlockSpec((None, None, PROJ_TM, width), lambda b, c, i: (b, c, i, 0))

    tab = pl.BlockSpec((None, PROJ_TM, LANES), lambda b, c, i: (c, inner(i), 0))
    sds = jax.ShapeDtypeStruct
    a_aug = (A_HEADS // 2) * PAIR_W
    return pl.pallas_call(
        functools.partial(_ab_in_body, nblk=nblk),
        out_shape=(sds((bsz, N_CLASS, tc, A_W), BF16),
                   sds((bsz, N_CLASS, tc + 2 * PROJ_TM, a_aug), BF16),
                   sds((bsz, N_CLASS, tc + 2 * PROJ_TM, a_aug), BF16),
                   sds((bsz, N_CLASS, tc, BQ_W), BF16),
                   sds((bsz, N_CLASS, tc, BKV_W), BF16),
                   sds((bsz, N_CLASS, tc, PAIR_W), BF16)),
        grid=(bsz, N_CLASS, nblk + 2),
        in_specs=[cm(D_MODEL), _resident((1, D_MODEL)), _resident(w.shape), _resident((1, BQ_W)),
                  _resident((1, BKV_W)), tab, tab, _resident(gq.shape), _resident(gk.shape)],
        out_specs=(cm(A_W), cm_halo(a_aug), cm_halo(a_aug), cm(BQ_W), cm(BKV_W), cm(PAIR_W)),
        compiler_params=_cparams(("parallel", "parallel", "arbitrary")),
        name="ab_in_proj",
    )(h, g, w, qg, kg, cos_t, sin_t, gq, gk)


def _aug_q(q, keep):
    e0 = jnp.where(_lane_iota(q.shape[0]) == 0, 1.0, 0.0).astype(BF16)
    return jnp.concatenate([jnp.where(keep, q, jnp.zeros_like(q)), e0], axis=1)


def _attn_a_body(q_ref, k_ref, v_ref, ts_ref, to_ref, o_ref):
    base = pl.program_id(2) * (A_NQ * A_TQ)
    lower = _lane_iota(A_TQ) < HEAD_DIM
    upper = jnp.logical_not(lower)

    def windows(qb, c):
        u0 = base + qb * A_TQ
        s_start = pl.multiple_of(u0 + (PROJ_TM - 256), A_TQ)
        o_start = pl.multiple_of(u0 + (PROJ_TM - A_OTH_OFF), 32)
        wins = [(c, s_start, A_SAME_W, ts_ref[...])]
        for cp in range(N_CLASS):
            if cp != c:
                wins.append((cp, o_start, A_OTH_W, to_ref[cp - c + (N_CLASS - 1)]))
        return wins

    def scores(qb, c):
        q = q_ref[c, qb * A_TQ:(qb + 1) * A_TQ, :]
        qp = jnp.concatenate([_aug_q(q, lower), _aug_q(q, upper)], axis=0)
        return [_dot_nt(k_ref[cp, pl.ds(st, w), :], qp) + tab for cp, st, w, tab in windows(qb, c)]

    def probs(ss):
        m = jnp.max(ss[0], axis=0, keepdims=True)
        for s in ss[1:]:
            m = jnp.maximum(m, jnp.max(s, axis=0, keepdims=True))
        return [jnp.exp2(s - m).astype(BF16) for s in ss]

    def output(qb, c, ps):
        acc = None
        for (cp, st, w, _), p in zip(windows(qb, c), ps):
            d = _dot_tn(v_ref[cp, pl.ds(st, w), 0:2 * A_V_ROWS], p)
            acc = d if acc is None else acc + d
        lo = acc[0:HEAD_DIM, 0:A_TQ] / acc[HEAD_DIM:HEAD_DIM + 1, 0:A_TQ]
        hi = (acc[A_V_ROWS:A_V_ROWS + HEAD_DIM, A_TQ:]
              / acc[A_V_ROWS + HEAD_DIM:A_V_ROWS + HEAD_DIM + 1, A_TQ:])
        o_ref[c, qb * A_TQ:(qb + 1) * A_TQ, :] = jnp.concatenate([lo, hi], axis=0).T.astype(o_ref.dtype)

    items = [(qb, c) for qb in range(A_NQ) for c in range(N_CLASS)]
    ss, ps = {}, {}
    for i in range(len(items) + A_LAG_PV):
        if i < len(items):
            ss[i] = scores(*items[i])
        if A_LAG_SOFTMAX <= i < len(items) + A_LAG_SOFTMAX:
            ps[i - A_LAG_SOFTMAX] = probs(ss.pop(i - A_LAG_SOFTMAX))
        if i >= A_LAG_PV:
            output(*items[i - A_LAG_PV], ps.pop(i - A_LAG_PV))


def _attn_a(aq, ak, av, tab_same, tab_oth):
    bsz, _, tc, _ = aq.shape
    qspec = pl.BlockSpec((None, N_CLASS, A_NQ * A_TQ, LANES), lambda b, p, i: (b, 0, i, p))
    kvspec = _kv_spec((None, N_CLASS, tc + 2 * PROJ_TM, PAIR_W), lambda b, p, i: (b, 0, 0, p))
    return pl.pallas_call(
        _attn_a_body,
        out_shape=jax.ShapeDtypeStruct(aq.shape, BF16),
        grid=(bsz, A_HEADS // 2, tc // (A_NQ * A_TQ)),
        in_specs=[qspec, kvspec, kvspec,
                  pl.BlockSpec((None, A_SAME_W, 2 * A_TQ), lambda b, p, i: (p, 0, 0)),
                  pl.BlockSpec((None, 2 * N_CLASS - 1, A_OTH_W, 2 * A_TQ), lambda b, p, i: (p, 0, 0, 0))],
        out_specs=qspec,
        compiler_params=_cparams(("parallel", "parallel", "arbitrary")),
        name="attn_dilated",
    )(aq, ak, av, tab_same, tab_oth)


def _attn_b_body(q_ref, k_ref, v_ref, o_ref, qp_ref, m_ref, acc_ref, s_ref, smax_ref, *, t):
    grp = B_Q_HEADS // B_KV_HEADS
    lower = _lane_iota(B_TQ) < HEAD_DIM
    zero = jnp.zeros((B_TQ, LANES), BF16)
    for j in range(grp):
        qb = q_ref[:, j * LANES:(j + 1) * LANES]
        qp_ref[0, j * B_TQ:(j + 1) * B_TQ, :] = jnp.where(lower, qb, zero)
        qp_ref[1, j * B_TQ:(j + 1) * B_TQ, :] = jnp.where(lower, zero, qb)
    m_ref[...] = jnp.full(m_ref.shape, NEG, F32)
    acc_ref[...] = jnp.zeros(acc_ref.shape, F32)
    n_chunk = B_KV_HEADS * grp
    n_steps = t // B_TK

    def cols(c):
        g, j = divmod(c, grp)
        return g, slice(j * B_TQ, (j + 1) * B_TQ)

    def rows(kb):
        return pl.ds(pl.multiple_of(kb * B_TK, B_TK), B_TK)

    def qk(k, c, slot):
        g, cs = cols(c)
        s = _dot_nt(k, qp_ref[g, cs, :])
        s_ref[slot, c] = s
        smax_ref[slot, c] = jnp.max(s, axis=0, keepdims=True)

    def softmax(c, slot):
        g, cs = cols(c)
        s = s_ref[slot, c]
        m_old = m_ref[g, :, cs]
        m_new = jnp.maximum(m_old, smax_ref[slot, c])
        m_ref[g, :, cs] = m_new
        return jnp.exp2(s - m_new).astype(BF16), jnp.exp2(m_old - m_new)

    def pv(v, c, p, alpha):
        g, cs = cols(c)
        acc_ref[g, :, cs] = alpha * acc_ref[g, :, cs] + _dot_tn(v[:, g * LANES:(g + 1) * LANES], p)

    def block(kb, slot, with_next):
        v = v_ref[rows(kb), :]
        k_next = k_ref[rows(kb + 1), :] if with_next else None
        pp = {}
        for c in range(n_chunk + B_LAG_PV):
            if c < n_chunk:
                pp[c] = softmax(c, slot)
                if with_next:
                    qk(k_next, c, 1 - slot)
            if c >= B_LAG_PV:
                pv(v, c - B_LAG_PV, *pp.pop(c - B_LAG_PV))

    k0 = k_ref[rows(0), :]
    for c in range(n_chunk):
        qk(k0, c, 0)

    unroll = B_UNROLL if n_steps >= 4 * B_UNROLL else 2

    def trip(i, carry):
        for u in range(unroll):
            block(unroll * i + u, u % 2, True)
        return carry

    lax.fori_loop(0, n_steps // unroll - 1, trip, 0)
    for u in range(unroll):
        block(n_steps - unroll + u, u % 2, u < unroll - 1)
    for j in range(grp):
        cs = slice(j * B_TQ, (j + 1) * B_TQ)
        x = jnp.concatenate([acc_ref[g, 0:HEAD_DIM, cs] / acc_ref[g, HEAD_DIM:HEAD_DIM + 1, cs]
                             for g in range(B_KV_HEADS)], axis=0)
        o_ref[:, j * LANES:(j + 1) * LANES] = x.T.astype(o_ref.dtype)


def _attn_b(bq, bk, bv):
    bsz, t, _ = bq.shape
    rows = (B_Q_HEADS // B_KV_HEADS) * B_TQ
    qspec = pl.BlockSpec((None, B_TQ, BQ_W), lambda b, i: (b, i, 0))
    return pl.pallas_call(
        functools.partial(_attn_b_body, t=t),
        out_shape=jax.ShapeDtypeStruct(bq.shape, BF16),
        grid=(bsz, t // B_TQ),
        in_specs=[qspec,
                  pl.BlockSpec((None, t, BKV_W), lambda b, i: (b, 0, 0)),
                  pl.BlockSpec((None, t, PAIR_W), lambda b, i: (b, 0, 0))],
        out_specs=qspec,
        scratch_shapes=[pltpu.VMEM((B_KV_HEADS, rows, LANES), BF16),
                        pltpu.VMEM((B_KV_HEADS, 1, rows), F32),
                        pltpu.VMEM((B_KV_HEADS, LANES, rows), F32),
                        pltpu.VMEM((2, B_Q_HEADS, B_TK, B_TQ), F32),
                        pltpu.VMEM((2, B_Q_HEADS, 1, B_TQ), F32)],
        compiler_params=_cparams(("parallel", "arbitrary")),
        name="attn_global",
    )(bq, bk, bv)


def _c_in_body(h_ref, g_ref, w_ref, q_ref, k_ref, v_ref, *, nblk):
    i = pl.program_id(1)
    is_halo = jnp.logical_or(i == 0, i == nblk + 1)

    @pl.when(is_halo)
    def _():
        _store_k_halo(k_ref)
        v_ref[...] = jnp.zeros(v_ref.shape, BF16)

    @pl.when(jnp.logical_not(is_halo))
    def _():
        u = _rms(h_ref[...], g_ref[...]).astype(BF16)
        proj = _dot(u, w_ref[...])
        q_ref[...] = (proj[:, 0:CQ_W] * QSCALE).astype(BF16)
        _store_k_aug(k_ref, proj[:, CQ_W:CQ_W + CKV_W], C_KV_HEADS // 2)
        _store_v_aug(v_ref, proj[:, CQ_W + CKV_W:CQ_W + 2 * CKV_W], C_KV_HEADS // 2)


def _c_in(h, g, w):
    bsz, t, _ = h.shape
    nblk = t // PROJ_TM
    aug = (C_KV_HEADS // 2) * PAIR_W

    def inner(i):
        return jnp.clip(i - 1, 0, nblk - 1)

    sds = jax.ShapeDtypeStruct
    return pl.pallas_call(
        functools.partial(_c_in_body, nblk=nblk),
        out_shape=(sds((bsz, t, CQ_W), BF16),
                   sds((bsz, t + 2 * PROJ_TM, aug), BF16),
                   sds((bsz, t + 2 * PROJ_TM, aug), BF16)),
        grid=(bsz, nblk + 2),
        in_specs=[pl.BlockSpec((None, PROJ_TM, D_MODEL), lambda b, i: (b, inner(i), 0)),
                  _resident((1, D_MODEL)), _resident(w.shape)],
        out_specs=(pl.BlockSpec((None, PROJ_TM, CQ_W), lambda b, i: (b, inner(i), 0)),
                   pl.BlockSpec((None, PROJ_TM, aug), lambda b, i: (b, i, 0)),
                   pl.BlockSpec((None, PROJ_TM, aug), lambda b, i: (b, i, 0))),
        compiler_params=_cparams(("parallel", "arbitrary")),
        name="c_in_proj",
    )(h, g, w)


def _attn_c_body(q_ref, k_ref, v_ref, tab_ref, sink_ref, o_ref):
    base = pl.program_id(2) * (C_NQ * C_TQ)
    grp = C_Q_HEADS // C_KV_HEADS
    lower = _lane_iota(C_TQ) < HEAD_DIM
    n_chunk = grp

    def cols(ch):
        return slice(ch * 2 * C_TQ, (ch + 1) * 2 * C_TQ)

    def kv_rows(qb):
        return pl.ds(pl.multiple_of(base + qb * C_TQ + (PROJ_TM - C_TQ), C_TQ), C_KW)

    def scores(qb, ch):
        half, js = ch // 2, (2 * (ch % 2), 2 * (ch % 2) + 1)
        keep = lower if half == 0 else jnp.logical_not(lower)
        qs = slice(qb * C_TQ, (qb + 1) * C_TQ)
        qp = jnp.concatenate([_aug_q(q_ref[qs, j * LANES:(j + 1) * LANES], keep) for j in js], axis=0)
        return _dot_nt(k_ref[kv_rows(qb), :], qp) + tab_ref[:, cols(ch)]

    def probs(ch, s):
        sink = sink_ref[:, cols(ch)]
        m = jnp.maximum(jnp.max(s, axis=0, keepdims=True), sink)
        return jnp.exp2(s - m).astype(BF16), jnp.exp2(sink - m)

    def output(qb, ch, p, sink_p):
        half = ch // 2
        acc = _dot_tn(v_ref[kv_rows(qb), half * LANES:(half + 1) * LANES], p)
        o = acc[0:HEAD_DIM] / (acc[HEAD_DIM:HEAD_DIM + 1] + sink_p)
        return [o[:, 0:C_TQ], o[:, C_TQ:]]

    items = [(qb, ch) for qb in range(C_NQ) for ch in range(n_chunk)]
    heads = {qb: [] for qb in range(C_NQ)}
    ss, ps = {}, {}
    for i in range(len(items) + C_LAG_PV):
        if i < len(items):
            ss[i] = scores(*items[i])
        if C_LAG_SOFTMAX <= i < len(items) + C_LAG_SOFTMAX:
            ps[i - C_LAG_SOFTMAX] = probs(items[i - C_LAG_SOFTMAX][1], ss.pop(i - C_LAG_SOFTMAX))
        if i >= C_LAG_PV:
            qb, ch = items[i - C_LAG_PV]
            heads[qb] += output(qb, ch, *ps.pop(i - C_LAG_PV))
            if ch == n_chunk - 1:
                for j in range(grp):
                    x = jnp.concatenate([heads[qb][j], heads[qb][grp + j]], axis=0)
                    o_ref[qb * C_TQ:(qb + 1) * C_TQ, j * LANES:(j + 1) * LANES] = x.T.astype(o_ref.dtype)


def _attn_c(cq, ck, cv, tab, sink_row):
    bsz, t, _ = cq.shape
    n_pair = C_KV_HEADS // 2
    cols = (C_Q_HEADS // n_pair) * C_TQ
    qspec = pl.BlockSpec((None, C_NQ * C_TQ, CQ_W // n_pair), lambda b, p, i: (b, i, p))
    kvspec = _kv_spec((None, t + 2 * PROJ_TM, PAIR_W), lambda b, p, i: (b, 0, p))
    return pl.pallas_call(
        _attn_c_body,
        out_shape=jax.ShapeDtypeStruct(cq.shape, BF16),
        grid=(bsz, n_pair, t // (C_NQ * C_TQ)),
        in_specs=[qspec, kvspec, kvspec,
                  pl.BlockSpec((None, C_KW, cols), lambda b, p, i: (p, 0, 0)),
                  pl.BlockSpec((None, 1, cols), lambda b, p, i: (p, 0, 0))],
        out_specs=qspec,
        compiler_params=_cparams(("parallel", "parallel", "arbitrary")),
        name="attn_banded",
    )(cq, ck, cv, tab, sink_row)


def _alibi_slopes(n):
    return (2.0 ** (-8.0 * (np.arange(n) + 1) / n)).astype(np.float32).astype(np.float64)


def _dilated_tables():
    slopes = _alibi_slopes(A_HEADS)[:, None, None]

    def table(off):
        mult = np.zeros(off.shape)
        for w, d in A_BRANCHES:
            mult += (off % d == 0) & (np.abs(off) <= w // 2)
        bias = (-slopes * np.abs(off)[None] + np.log(np.maximum(mult, 1.0))[None]) * LOG2E
        tab = np.where(mult[None] > 0, bias, NEG).astype(np.float32)
        tab = tab.reshape(A_HEADS // 2, 2, off.shape[0], A_TQ).transpose(0, 2, 1, 3)
        return tab.reshape(A_HEADS // 2, off.shape[0], 2 * A_TQ)

    qi = np.arange(A_TQ)[None, :]
    same = table(N_CLASS * (np.arange(A_SAME_W)[:, None] - 256 - qi))
    oth = []
    for delta in range(-(N_CLASS - 1), N_CLASS):
        off = N_CLASS * (np.arange(A_OTH_W)[:, None] - A_OTH_OFF - qi) + delta
        oth.append(table(off) if delta != 0 else np.full((A_HEADS // 2, A_OTH_W, 2 * A_TQ), NEG, np.float32))
    return jnp.asarray(same), jnp.asarray(np.stack(oth, axis=1))


def _c_head(p, half, j):
    grp = C_Q_HEADS // C_KV_HEADS
    return 2 * grp * p + grp * half + j


def _c_head_order():
    grp = C_Q_HEADS // C_KV_HEADS
    return [_c_head(p, half, j) for p in range(C_KV_HEADS // 2) for j in range(grp) for half in range(2)]


def _banded_table():
    slopes = _alibi_slopes(C_Q_HEADS)
    grp = C_Q_HEADS // C_KV_HEADS
    rel = (np.arange(C_KW)[:, None] - C_TQ) - np.arange(C_TQ)[None, :]
    tab = np.zeros((C_KV_HEADS // 2, C_KW, 2 * grp, C_TQ), np.float32)
    for p in range(C_KV_HEADS // 2):
        for half in range(2):
            for j in range(grp):
                bias = -slopes[_c_head(p, half, j)] * np.abs(rel) * LOG2E
                tab[p, :, half * grp + j, :] = np.where(np.abs(rel) <= C_WINDOW, bias, NEG)
    return jnp.asarray(tab.reshape(C_KV_HEADS // 2, C_KW, 2 * grp * C_TQ))


def _rope_tables(t):
    n_rows = t // GRID_W
    half = HEAD_DIM // 2
    row = jnp.repeat(jnp.arange(n_rows), GRID_W).astype(F32)
    col = jnp.tile(jnp.arange(GRID_W), n_rows).astype(F32)
    fr = ROPE_THETA ** (-jnp.arange(0, half, 2, dtype=F32) / half)
    ang = jnp.concatenate([row[:, None] * fr, col[:, None] * fr], axis=-1)
    cos, sin = jnp.cos(ang), jnp.sin(ang)
    cos_h = jnp.concatenate([cos, cos], axis=-1)
    sin_h = jnp.concatenate([-sin, sin], axis=-1)

    def cm(x):
        x = jnp.concatenate([x, x], axis=-1)
        return x.reshape(t // N_CLASS, N_CLASS, LANES).transpose(1, 0, 2)

    return cm(cos_h), cm(sin_h)


_DEINT = np.concatenate([np.arange(0, HEAD_DIM, 2), np.arange(1, HEAD_DIM, 2)])
_B_HEAD_ORDER = [0, 4, 1, 5, 2, 6, 3, 7]


def _head_cols(heads, base, dims):
    return np.concatenate([base + h * HEAD_DIM + dims for h in heads])


def _prep_ab_weights(w_in, w_out, q_gain, k_gain):
    nat = np.arange(HEAD_DIM)
    o = 3 * A_W
    cols = np.concatenate([
        np.arange(o),
        _head_cols(_B_HEAD_ORDER, o, _DEINT),
        _head_cols(range(B_KV_HEADS), o + BQ_W, _DEINT),
        np.arange(o + BQ_W + BKV_W, o + BQ_W + 2 * BKV_W)])
    w = w_in[:, cols].astype(BF16)
    wa = w_out[:A_W].astype(BF16)
    wb = w_out[_head_cols(_B_HEAD_ORDER, A_W, nat)].astype(BF16)
    qg = jnp.tile(q_gain[_DEINT], B_Q_HEADS)[None, :]
    kg = jnp.tile(k_gain[_DEINT], B_KV_HEADS)[None, :]
    return w, wa, wb, qg, kg


def _prep_c_weights(w_in, w_out, sink):
    nat = np.arange(HEAD_DIM)
    qcols = _head_cols(_c_head_order(), 0, nat)
    cols = np.concatenate([qcols, np.arange(CQ_W, CQ_W + 2 * CKV_W)])
    w = w_in[:, cols].astype(BF16)
    wo = w_out[qcols].astype(BF16)
    grp = C_Q_HEADS // C_KV_HEADS
    idx = np.array([[_c_head(p, half, j) for half in range(2) for j in range(grp)]
                    for p in range(C_KV_HEADS // 2)])
    sink_row = jnp.repeat(sink[idx] * LOG2E, C_TQ, axis=-1)[:, None, :].astype(F32)
    return w, wo, sink_row


def _block_ones(width):
    g = np.arange(width) // HEAD_DIM
    return jnp.asarray((g[:, None] == g[None, :]).astype(np.float32)).astype(BF16)


def _trunk(x, p):
    bsz, t, _ = x.shape
    tc = t // N_CLASS
    cos_t, sin_t = _rope_tables(t)
    ffn_w = p["ffn_w"]

    h = _ffn(x, [], p["g"][0][0], ffn_w, 0, p["fg"], cm_out=True)
    aq, ak, av, bq, bk, bv = _ab_in(h, p["g"][0][1], p["ab_w"], p["qg"], p["kg"],
                                    cos_t, sin_t, p["gq"], p["gk"])
    a_out = _attn_a(aq, ak, av, p["tab_same"], p["tab_oth"])
    b_out = _attn_b(bq.reshape(bsz, t, BQ_W), bk.reshape(bsz, t, BKV_W), bv.reshape(bsz, t, PAIR_W))
    mix = [(a_out, p["ab_wa"]), (b_out.reshape(bsz, N_CLASS, tc, BQ_W), p["ab_wb"])]
    h = _ffn(h, mix, p["g"][0][2], ffn_w, 1, p["fg"], cm_in=True)

    h = _ffn(h, [], p["g"][1][0], ffn_w, 2, p["fg"])
    cq, ck, cv = _c_in(h, p["g"][1][1], p["c_w"])
    c_out = _attn_c(cq, ck, cv, p["tab_c"], p["sink"])
    return _ffn(h, [(c_out, p["c_wo"])], p["g"][1][2], ffn_w, 3, p["fg"], final=True)


def kernel(x_prompt, x_sample, norm_g, ffn_w1, ffn_w3, ffn_w2, ab_w_in, ab_w_out, ab_q_gain, ab_k_gain,
           c_w_in, c_w_out, c_sink, final_g):
    tab_same, tab_oth = _dilated_tables()
    ab_w, ab_wa, ab_wb, qg, kg = _prep_ab_weights(ab_w_in[0], ab_w_out[0], ab_q_gain[0], ab_k_gain[0])
    c_w, c_wo, sink = _prep_c_weights(c_w_in[0], c_w_out[0], c_sink[0])
    stack = lambda w: w.reshape((-1,) + w.shape[2:]).astype(BF16)
    p = {
        "g": [[norm_g[l, k][None, :] for k in range(3)] for l in range(2)],
        "ffn_w": (stack(ffn_w1), stack(ffn_w3), stack(ffn_w2)),
        "fg": final_g[None, :],
        "ab_w": ab_w, "ab_wa": ab_wa, "ab_wb": ab_wb, "qg": qg, "kg": kg,
        "gq": _block_ones(BQ_W), "gk": _block_ones(BKV_W),
        "tab_same": tab_same, "tab_oth": tab_oth,
        "c_w": c_w, "c_wo": c_wo, "sink": sink, "tab_c": _banded_table(),
    }
    return _trunk(x_prompt, p), _trunk(x_sample, p)
```

```python
import functools

import numpy as np
import jax
import jax.numpy as jnp
from jax import lax
from jax.experimental import pallas as pl
from jax.experimental.pallas import tpu as pltpu

F32 = jnp.float32
BF16 = jnp.bfloat16

D_MODEL = 1024
HEAD_DIM = 64
D_FF = 2816
EPS = 1e-6
NEG = -1e30
GRID_W = 64
ROPE_THETA = 10000.0
A_HEADS = 8
A_BRANCHES = ((128, 1), (512, 4), (2048, 16))
B_Q_HEADS = 8
B_KV_HEADS = 2
C_Q_HEADS = 16
C_KV_HEADS = 4
C_WINDOW = 128
A_W = A_HEADS * HEAD_DIM
BQ_W = B_Q_HEADS * HEAD_DIM
BKV_W = B_KV_HEADS * HEAD_DIM
CQ_W = C_Q_HEADS * HEAD_DIM
CKV_W = C_KV_HEADS * HEAD_DIM
LOG2E = 1.4426950408889634
QSCALE = HEAD_DIM ** -0.5 * LOG2E

LANES = 128
PAIR_W = 2 * LANES
N_CLASS = 4
VMEM_LIMIT = 56 * 1024 * 1024

FFN_TM = 512
FF_CHUNK = 256
PROJ_TM = 512
A_TQ = 128
A_NQ = 4
A_V_ROWS = 80
A_LAG_SOFTMAX = 2
A_LAG_PV = 4
A_SAME_W = 640
A_OTH_W = 192
A_OTH_OFF = 32
B_TQ = 256
B_TK = 512
B_UNROLL = 4
B_LAG_PV = 1
A_KV_VMEM_BYTES = 32 * 1024 * 1024
C_KV_VMEM_BYTES = 36 * 1024 * 1024
AB_ROW_SPLIT = 4
C_TQ = 128
C_NQ = 8
C_LAG_SOFTMAX = 2
C_LAG_PV = 4
C_KW = 3 * C_TQ


def _cparams(sem):
    return pltpu.CompilerParams(dimension_semantics=sem, vmem_limit_bytes=VMEM_LIMIT)


def _rms(x, g):
    return x * lax.rsqrt(jnp.mean(x * x, axis=-1, keepdims=True) + EPS) * g


def _dot(a, b):
    return jnp.dot(a, b, preferred_element_type=F32)


def _dot_nt(a, b):
    return lax.dot_general(a, b, (((1,), (1,)), ((), ())), preferred_element_type=F32)


def _dot_tn(a, b):
    return lax.dot_general(a, b, (((0,), (0,)), ((), ())), preferred_element_type=F32)


def _resident(shape):
    nd = len(shape)
    return pl.BlockSpec(shape, lambda *_: (0,) * nd, pipeline_mode=pl.Buffered(1))


def _kv_specs(block_shape, index_map, budget):
    nbytes = 2 * int(np.prod([d for d in block_shape if d is not None]))
    n_double = 2 if 4 * nbytes <= budget else (1 if 3 * nbytes <= budget else 0)
    single = pl.BlockSpec(block_shape, index_map, pipeline_mode=pl.Buffered(1))
    double = pl.BlockSpec(block_shape, index_map)
    return (double if n_double >= 1 else single), (double if n_double == 2 else single)


def _lane_iota(rows):
    return lax.broadcasted_iota(jnp.int32, (rows, LANES), 1)


def _ffn_body(*refs, final, cm_in, cm_out, n_mix):
    h_ref = refs[0]
    mix_refs = refs[1:1 + n_mix]
    mixw_refs = refs[1 + n_mix:1 + 2 * n_mix]
    g_ref, w1_ref, w3_ref, w2_ref, fg_ref, o_ref = refs[1 + 2 * n_mix:7 + 2 * n_mix]
    scratch = refs[7 + 2 * n_mix:]
    sub = FFN_TM // N_CLASS
    n_col = D_MODEL // LANES
    x = h_ref[...].reshape(FFN_TM, D_MODEL) if cm_in else h_ref[...]
    for a_ref, w_ref in zip(mix_refs, mixw_refs):
        a = a_ref[...]
        x = x + _dot(a.reshape(FFN_TM, a.shape[-1]) if cm_in else a, w_ref[...])
    n = _rms(x, g_ref[...]).astype(BF16)
    acc = jnp.zeros(x.shape, F32)
    for c in range(D_FF // FF_CHUNK):
        sl = slice(c * FF_CHUNK, (c + 1) * FF_CHUNK)
        a = _dot(n, w1_ref[:, sl])
        b = _dot(n, w3_ref[:, sl])
        hm = (a * jax.nn.sigmoid(a) * b).astype(BF16)
        acc = acc + _dot(hm, w2_ref[sl, :])
    y = x + 0.5 * acc
    if final:
        y = _rms(y, fg_ref[...])
    if cm_out:
        perm_ref, = scratch
        for j in range(n_col):
            perm_ref[j] = y[:, j * LANES:(j + 1) * LANES]
        for c in range(N_CLASS):
            for j in range(n_col):
                o_ref[c, :, j * LANES:(j + 1) * LANES] = perm_ref[j, pl.ds(c, sub, stride=N_CLASS), :]
    elif cm_in:
        perm_ref, = scratch
        for c in range(N_CLASS):
            for j in range(n_col):
                perm_ref[j, pl.ds(c, sub, stride=N_CLASS), :] = y[c * sub:(c + 1) * sub, j * LANES:(j + 1) * LANES]
        o_ref[...] = jnp.concatenate([perm_ref[j] for j in range(n_col)], axis=1)
    else:
        o_ref[...] = y


def _ffn(h, mix, g, ffn_w, idx, fg, *, final=False, cm_in=False, cm_out=False):
    bsz = h.shape[0]
    t = h.shape[1] * h.shape[2] if cm_in else h.shape[1]
    tc = t // N_CLASS
    sub = FFN_TM // N_CLASS

    def rows(width, class_major):
        if class_major:
            return pl.BlockSpec((None, N_CLASS, sub, width), lambda b, i: (b, 0, i, 0))
        return pl.BlockSpec((None, FFN_TM, width), lambda b, i: (b, i, 0))

    def weight(w):
        return pl.BlockSpec((None,) + w.shape[1:], lambda b, i: (idx, 0, 0), pipeline_mode=pl.Buffered(1))

    acts = [a for a, _ in mix]
    ws = [w for _, w in mix]
    out_shape = (bsz, N_CLASS, tc, D_MODEL) if cm_out else (bsz, t, D_MODEL)
    return pl.pallas_call(
        functools.partial(_ffn_body, final=final, cm_in=cm_in, cm_out=cm_out, n_mix=len(mix)),
        out_shape=jax.ShapeDtypeStruct(out_shape, F32),
        grid=(bsz, t // FFN_TM),
        in_specs=([rows(D_MODEL, cm_in)] + [rows(a.shape[-1], cm_in) for a in acts]
                  + [_resident(w.shape) for w in ws]
                  + [_resident((1, D_MODEL))] + [weight(w) for w in ffn_w] + [_resident((1, D_MODEL))]),
        out_specs=rows(D_MODEL, cm_out),
        scratch_shapes=[pltpu.VMEM((D_MODEL // LANES, FFN_TM, LANES), F32)] if (cm_in or cm_out) else [],
        compiler_params=_cparams(("parallel", "parallel")),
        name="ffn_final" if final else "ffn",
    )(h, *acts, *ws, g, *ffn_w, fg)


def _store_k_aug(k_ref, kf, n_pair):
    zeros = jnp.zeros((kf.shape[0], LANES), BF16)
    for p in range(n_pair):
        k_ref[:, p * PAIR_W:p * PAIR_W + LANES] = kf[:, p * LANES:(p + 1) * LANES].astype(BF16)
        k_ref[:, p * PAIR_W + LANES:(p + 1) * PAIR_W] = zeros


def _store_k_halo(k_ref):
    lane = lax.broadcasted_iota(jnp.int32, k_ref.shape, 1)
    k_ref[...] = jnp.where(lane % PAIR_W == LANES, NEG, 0.0).astype(BF16)


def _store_v_aug(v_ref, vf, n_pair):
    lane = _lane_iota(vf.shape[0])
    lower = lane < HEAD_DIM
    one = jnp.where(lane == HEAD_DIM, 1.0, 0.0).astype(F32)
    for p in range(n_pair):
        vp = vf[:, p * LANES:(p + 1) * LANES]
        v_ref[:, p * PAIR_W:p * PAIR_W + LANES] = jnp.where(lower, vp, one).astype(BF16)
        v_ref[:, p * PAIR_W + LANES:(p + 1) * PAIR_W] = (
            jnp.where(lower, pltpu.roll(vp, HEAD_DIM, 1), one).astype(BF16))


def _store_v_packed(v_ref, vf, n_pair):
    lane = _lane_iota(vf.shape[0])
    pad = A_V_ROWS - HEAD_DIM
    for p in range(n_pair):
        vp = vf[:, p * LANES:(p + 1) * LANES]
        shifted = pltpu.roll(vp, pad, 1)
        first = jnp.where(lane < HEAD_DIM, vp,
                          jnp.where(lane == HEAD_DIM, 1.0, jnp.where(lane >= A_V_ROWS, shifted, 0.0)))
        second = jnp.where(lane < pad, shifted, jnp.where(lane == pad, 1.0, 0.0))
        v_ref[:, p * PAIR_W:p * PAIR_W + LANES] = first.astype(BF16)
        v_ref[:, p * PAIR_W + LANES:(p + 1) * PAIR_W] = second.astype(BF16)


def _group_mean_sq(x, gmat):
    ss = x * x
    hi = ss.astype(BF16)
    lo = (ss - hi.astype(F32)).astype(BF16)
    return (_dot(hi, gmat) + _dot(lo, gmat)) * (1.0 / HEAD_DIM)


def _rope(y, cos, sin_signed):
    first_half = (_lane_iota(y.shape[0]) & (HEAD_DIM // 2)) == 0
    outs = []
    for j in range(y.shape[1] // LANES):
        yb = y[:, j * LANES:(j + 1) * LANES]
        partner = jnp.where(first_half,
                            pltpu.roll(yb, LANES - HEAD_DIM // 2, 1),
                            pltpu.roll(yb, HEAD_DIM // 2, 1))
        outs.append(yb * cos + partner * sin_signed)
    return outs[0] if len(outs) == 1 else jnp.concatenate(outs, axis=1)


def _ab_in_body(h_ref, g_ref, w_ref, qg_ref, kg_ref, cos_ref, sin_ref, gq_ref, gk_ref,
                aq_ref, ak_ref, av_ref, bq_ref, bk_ref, bv_ref, *, nblk):
    i = pl.program_id(2)
    is_halo = jnp.logical_or(i == 0, i == nblk + 1)

    @pl.when(is_halo)
    def _():
        _store_k_halo(ak_ref)
        av_ref[...] = jnp.zeros(av_ref.shape, BF16)

    @pl.when(jnp.logical_not(is_halo))
    def _():
        o = 3 * A_W
        part = PROJ_TM // AB_ROW_SPLIT
        groups = [slice(r * part, (r + 1) * part) for r in range(AB_ROW_SPLIT)]
        us = [_rms(h_ref[rs, :], g_ref[...]).astype(BF16) for rs in groups]
        pbs = [_dot(u, w_ref[:, o:]) for u in us]
        pas = []
        for rs, u, pb in zip(groups, us, pbs):
            pas.append(_dot(u, w_ref[:, 0:o]))
            cos = cos_ref[rs, :]
            sin = sin_ref[rs, :]
            xq = pb[:, 0:BQ_W]
            yq = xq * lax.rsqrt(_group_mean_sq(xq, gq_ref[...]) + EPS) * qg_ref[...]
            bq_ref[rs, :] = (_rope(yq, cos, sin) * QSCALE).astype(BF16)
            xk = pb[:, BQ_W:BQ_W + BKV_W]
            yk = xk * lax.rsqrt(_group_mean_sq(xk, gk_ref[...]) + EPS) * kg_ref[...]
            bk_ref[rs, :] = _rope(yk, cos, sin).astype(BF16)
            _store_v_aug(bv_ref.at[rs, :], pb[:, BQ_W + BKV_W:BQ_W + 2 * BKV_W], B_KV_HEADS // 2)
        for rs, pa in zip(groups, pas):
            aq_ref[rs, :] = (pa[:, 0:A_W] * QSCALE).astype(BF16)
            _store_k_aug(ak_ref.at[rs, :], pa[:, A_W:2 * A_W], A_HEADS // 2)
            _store_v_packed(av_ref.at[rs, :], pa[:, 2 * A_W:3 * A_W], A_HEADS // 2)


def _ab_in(h, g, w, qg, kg, cos_t, sin_t, gq, gk):
    bsz, _, tc, _ = h.shape
    nblk = tc // PROJ_TM

    def inner(i):
        return jnp.clip(i - 1, 0, nblk - 1)

    def cm(width):
        return pl.BlockSpec((None, None, PROJ_TM, width), lambda b, c, i: (b, c, inner(i), 0))

    def cm_halo(width):
        return pl.BlockSpec((None, None, PROJ_TM, width), lambda b, c, i: (b, c, i, 0))

    tab = pl.BlockSpec((None, PROJ_TM, LANES), lambda b, c, i: (c, inner(i), 0))
    sds = jax.ShapeDtypeStruct
    a_aug = (A_HEADS // 2) * PAIR_W
    return pl.pallas_call(
        functools.partial(_ab_in_body, nblk=nblk),
        out_shape=(sds((bsz, N_CLASS, tc, A_W), BF16),
                   sds((bsz, N_CLASS, tc + 2 * PROJ_TM, a_aug), BF16),
                   sds((bsz, N_CLASS, tc + 2 * PROJ_TM, a_aug), BF16),
                   sds((bsz, N_CLASS, tc, BQ_W), BF16),
                   sds((bsz, N_CLASS, tc, BKV_W), BF16),
                   sds((bsz, N_CLASS, tc, PAIR_W), BF16)),
        grid=(bsz, N_CLASS, nblk + 2),
        in_specs=[cm(D_MODEL), _resident((1, D_MODEL)), _resident(w.shape), _resident((1, BQ_W)),
                  _resident((1, BKV_W)), tab, tab, _resident(gq.shape), _resident(gk.shape)],
        out_specs=(cm(A_W), cm_halo(a_aug), cm_halo(a_aug), cm(BQ_W), cm(BKV_W), cm(PAIR_W)),
        compiler_params=_cparams(("parallel", "parallel", "arbitrary")),
        name="ab_in_proj",
    )(h, g, w, qg, kg, cos_t, sin_t, gq, gk)


def _aug_q(q, keep):
    e0 = jnp.where(_lane_iota(q.shape[0]) == 0, 1.0, 0.0).astype(BF16)
    return jnp.concatenate([jnp.where(keep, q, jnp.zeros_like(q)), e0], axis=1)


def _attn_a_body(q_ref, k_ref, v_ref, ts_ref, to_ref, o_ref):
    base = pl.program_id(2) * (A_NQ * A_TQ)
    lower = _lane_iota(A_TQ) < HEAD_DIM
    upper = jnp.logical_not(lower)

    def windows(qb, c):
        u0 = base + qb * A_TQ
        s_start = pl.multiple_of(u0 + (PROJ_TM - 256), A_TQ)
        o_start = pl.multiple_of(u0 + (PROJ_TM - A_OTH_OFF), 32)
        wins = [(c, s_start, A_SAME_W, ts_ref[...])]
        for cp in range(N_CLASS):
            if cp != c:
                wins.append((cp, o_start, A_OTH_W, to_ref[cp - c + (N_CLASS - 1)]))
        return wins

    def scores(qb, c):
        q = q_ref[c, qb * A_TQ:(qb + 1) * A_TQ, :]
        qp = jnp.concatenate([_aug_q(q, lower), _aug_q(q, upper)], axis=0)
        return [_dot_nt(k_ref[cp, pl.ds(st, w), :], qp) + tab for cp, st, w, tab in windows(qb, c)]

    def probs(ss):
        m = jnp.max(ss[0], axis=0, keepdims=True)
        for s in ss[1:]:
            m = jnp.maximum(m, jnp.max(s, axis=0, keepdims=True))
        return [jnp.exp2(s - m).astype(BF16) for s in ss]

    def output(qb, c, ps):
        acc = None
        for (cp, st, w, _), p in zip(windows(qb, c), ps):
            d = _dot_tn(v_ref[cp, pl.ds(st, w), 0:2 * A_V_ROWS], p)
            acc = d if acc is None else acc + d
        lo = acc[0:HEAD_DIM, 0:A_TQ] / acc[HEAD_DIM:HEAD_DIM + 1, 0:A_TQ]
        hi = (acc[A_V_ROWS:A_V_ROWS + HEAD_DIM, A_TQ:]
              / acc[A_V_ROWS + HEAD_DIM:A_V_ROWS + HEAD_DIM + 1, A_TQ:])
        o_ref[c, qb * A_TQ:(qb + 1) * A_TQ, :] = jnp.concatenate([lo, hi], axis=0).T.astype(o_ref.dtype)

    items = [(qb, c) for qb in range(A_NQ) for c in range(N_CLASS)]
    ss, ps = {}, {}
    for i in range(len(items) + A_LAG_PV):
        if i < len(items):
            ss[i] = scores(*items[i])
        if A_LAG_SOFTMAX <= i < len(items) + A_LAG_SOFTMAX:
            ps[i - A_LAG_SOFTMAX] = probs(ss.pop(i - A_LAG_SOFTMAX))
        if i >= A_LAG_PV:
            output(*items[i - A_LAG_PV], ps.pop(i - A_LAG_PV))


def _attn_a(aq, ak, av, tab_same, tab_oth):
    bsz, _, tc, _ = aq.shape
    qspec = pl.BlockSpec((None, N_CLASS, A_NQ * A_TQ, LANES), lambda b, p, i: (b, 0, i, p))
    kspec, vspec = _kv_specs((None, N_CLASS, tc + 2 * PROJ_TM, PAIR_W), lambda b, p, i: (b, 0, 0, p),
                             A_KV_VMEM_BYTES)
    return pl.pallas_call(
        _attn_a_body,
        out_shape=jax.ShapeDtypeStruct(aq.shape, BF16),
        grid=(bsz, A_HEADS // 2, tc // (A_NQ * A_TQ)),
        in_specs=[qspec, kspec, vspec,
                  pl.BlockSpec((None, A_SAME_W, 2 * A_TQ), lambda b, p, i: (p, 0, 0)),
                  pl.BlockSpec((None, 2 * N_CLASS - 1, A_OTH_W, 2 * A_TQ), lambda b, p, i: (p, 0, 0, 0))],
        out_specs=qspec,
        compiler_params=_cparams(("parallel", "parallel", "arbitrary")),
        name="attn_dilated",
    )(aq, ak, av, tab_same, tab_oth)


def _attn_b_body(q_ref, k_ref, v_ref, o_ref, qp_ref, m_ref, acc_ref, s_ref, smax_ref, *, t):
    grp = B_Q_HEADS // B_KV_HEADS
    lower = _lane_iota(B_TQ) < HEAD_DIM
    zero = jnp.zeros((B_TQ, LANES), BF16)
    for j in range(grp):
        qb = q_ref[:, j * LANES:(j + 1) * LANES]
        qp_ref[0, j * B_TQ:(j + 1) * B_TQ, :] = jnp.where(lower, qb, zero)
        qp_ref[1, j * B_TQ:(j + 1) * B_TQ, :] = jnp.where(lower, zero, qb)
    m_ref[...] = jnp.full(m_ref.shape, NEG, F32)
    acc_ref[...] = jnp.zeros(acc_ref.shape, F32)
    n_chunk = B_KV_HEADS * grp
    n_steps = t // B_TK

    def cols(c):
        g, j = divmod(c, grp)
        return g, slice(j * B_TQ, (j + 1) * B_TQ)

    def rows(kb):
        return pl.ds(pl.multiple_of(kb * B_TK, B_TK), B_TK)

    def qk(k, c, slot):
        g, cs = cols(c)
        s = _dot_nt(k, qp_ref[g, cs, :])
        s_ref[slot, c] = s
        smax_ref[slot, c] = jnp.max(s, axis=0, keepdims=True)

    def softmax(c, slot):
        g, cs = cols(c)
        s = s_ref[slot, c]
        m_old = m_ref[g, :, cs]
        m_new = jnp.maximum(m_old, smax_ref[slot, c])
        m_ref[g, :, cs] = m_new
        return jnp.exp2(s - m_new).astype(BF16), jnp.exp2(m_old - m_new)

    def pv(v, c, p, alpha):
        g, cs = cols(c)
        acc_ref[g, :, cs] = alpha * acc_ref[g, :, cs] + _dot_tn(v[:, g * LANES:(g + 1) * LANES], p)

    def block(kb, slot, with_next):
        v = v_ref[rows(kb), :]
        k_next = k_ref[rows(kb + 1), :] if with_next else None
        pp = {}
        for c in range(n_chunk + B_LAG_PV):
            if c < n_chunk:
                pp[c] = softmax(c, slot)
                if with_next:
                    qk(k_next, c, 1 - slot)
            if c >= B_LAG_PV:
                pv(v, c - B_LAG_PV, *pp.pop(c - B_LAG_PV))

    k0 = k_ref[rows(0), :]
    for c in range(n_chunk):
        qk(k0, c, 0)

    unroll = B_UNROLL if n_steps >= 4 * B_UNROLL else 2

    def trip(i, carry):
        for u in range(unroll):
            block(unroll * i + u, u % 2, True)
        return carry

    lax.fori_loop(0, n_steps // unroll - 1, trip, 0)
    for u in range(unroll):
        block(n_steps - unroll + u, u % 2, u < unroll - 1)
    for j in range(grp):
        cs = slice(j * B_TQ, (j + 1) * B_TQ)
        x = jnp.concatenate([acc_ref[g, 0:HEAD_DIM, cs] / acc_ref[g, HEAD_DIM:HEAD_DIM + 1, cs]
                             for g in range(B_KV_HEADS)], axis=0)
        o_ref[:, j * LANES:(j + 1) * LANES] = x.T.astype(o_ref.dtype)


def _attn_b(bq, bk, bv):
    bsz, t, _ = bq.shape
    rows = (B_Q_HEADS // B_KV_HEADS) * B_TQ
    qspec = pl.BlockSpec((None, B_TQ, BQ_W), lambda b, i: (b, i, 0))
    return pl.pallas_call(
        functools.partial(_attn_b_body, t=t),
        out_shape=jax.ShapeDtypeStruct(bq.shape, BF16),
        grid=(bsz, t // B_TQ),
        in_specs=[qspec,
                  pl.BlockSpec((None, t, BKV_W), lambda b, i: (b, 0, 0)),
                  pl.BlockSpec((None, t, PAIR_W), lambda b, i: (b, 0, 0))],
        out_specs=qspec,
        scratch_shapes=[pltpu.VMEM((B_KV_HEADS, rows, LANES), BF16),
                        pltpu.VMEM((B_KV_HEADS, 1, rows), F32),
                        pltpu.VMEM((B_KV_HEADS, LANES, rows), F32),
                        pltpu.VMEM((2, B_Q_HEADS, B_TK, B_TQ), F32),
                        pltpu.VMEM((2, B_Q_HEADS, 1, B_TQ), F32)],
        compiler_params=_cparams(("parallel", "arbitrary")),
        name="attn_global",
    )(bq, bk, bv)


def _c_in_body(h_ref, g_ref, w_ref, q_ref, k_ref, v_ref, *, nblk):
    i = pl.program_id(1)
    is_halo = jnp.logical_or(i == 0, i == nblk + 1)

    @pl.when(is_halo)
    def _():
        _store_k_halo(k_ref)
        v_ref[...] = jnp.zeros(v_ref.shape, BF16)

    @pl.when(jnp.logical_not(is_halo))
    def _():
        u = _rms(h_ref[...], g_ref[...]).astype(BF16)
        proj = _dot(u, w_ref[...])
        q_ref[...] = (proj[:, 0:CQ_W] * QSCALE).astype(BF16)
        _store_k_aug(k_ref, proj[:, CQ_W:CQ_W + CKV_W], C_KV_HEADS // 2)
        _store_v_aug(v_ref, proj[:, CQ_W + CKV_W:CQ_W + 2 * CKV_W], C_KV_HEADS // 2)


def _c_in(h, g, w):
    bsz, t, _ = h.shape
    nblk = t // PROJ_TM
    aug = (C_KV_HEADS // 2) * PAIR_W

    def inner(i):
        return jnp.clip(i - 1, 0, nblk - 1)

    sds = jax.ShapeDtypeStruct
    return pl.pallas_call(
        functools.partial(_c_in_body, nblk=nblk),
        out_shape=(sds((bsz, t, CQ_W), BF16),
                   sds((bsz, t + 2 * PROJ_TM, aug), BF16),
                   sds((bsz, t + 2 * PROJ_TM, aug), BF16)),
        grid=(bsz, nblk + 2),
        in_specs=[pl.BlockSpec((None, PROJ_TM, D_MODEL), lambda b, i: (b, inner(i), 0)),
                  _resident((1, D_MODEL)), _resident(w.shape)],
        out_specs=(pl.BlockSpec((None, PROJ_TM, CQ_W), lambda b, i: (b, inner(i), 0)),
                   pl.BlockSpec((None, PROJ_TM, aug), lambda b, i: (b, i, 0)),
                   pl.BlockSpec((None, PROJ_TM, aug), lambda b, i: (b, i, 0))),
        compiler_params=_cparams(("parallel", "arbitrary")),
        name="c_in_proj",
    )(h, g, w)


def _attn_c_body(q_ref, k_ref, v_ref, tab_ref, sink_ref, o_ref):
    base = pl.program_id(2) * (C_NQ * C_TQ)
    grp = C_Q_HEADS // C_KV_HEADS
    lower = _lane_iota(C_TQ) < HEAD_DIM
    n_chunk = grp

    def cols(ch):
        return slice(ch * 2 * C_TQ, (ch + 1) * 2 * C_TQ)

    def kv_rows(qb):
        return pl.ds(pl.multiple_of(base + qb * C_TQ + (PROJ_TM - C_TQ), C_TQ), C_KW)

    def scores(qb, ch):
        half, js = ch // 2, (2 * (ch % 2), 2 * (ch % 2) + 1)
        keep = lower if half == 0 else jnp.logical_not(lower)
        qs = slice(qb * C_TQ, (qb + 1) * C_TQ)
        qp = jnp.concatenate([_aug_q(q_ref[qs, j * LANES:(j + 1) * LANES], keep) for j in js], axis=0)
        return _dot_nt(k_ref[kv_rows(qb), :], qp) + tab_ref[:, cols(ch)]

    def probs(ch, s):
        sink = sink_ref[:, cols(ch)]
        m = jnp.maximum(jnp.max(s, axis=0, keepdims=True), sink)
        return jnp.exp2(s - m).astype(BF16), jnp.exp2(sink - m)

    def output(qb, ch, p, sink_p):
        half = ch // 2
        acc = _dot_tn(v_ref[kv_rows(qb), half * LANES:(half + 1) * LANES], p)
        o = acc[0:HEAD_DIM] / (acc[HEAD_DIM:HEAD_DIM + 1] + sink_p)
        return [o[:, 0:C_TQ], o[:, C_TQ:]]

    items = [(qb, ch) for qb in range(C_NQ) for ch in range(n_chunk)]
    heads = {qb: [] for qb in range(C_NQ)}
    ss, ps = {}, {}
    for i in range(len(items) + C_LAG_PV):
        if i < len(items):
            ss[i] = scores(*items[i])
        if C_LAG_SOFTMAX <= i < len(items) + C_LAG_SOFTMAX:
            ps[i - C_LAG_SOFTMAX] = probs(items[i - C_LAG_SOFTMAX][1], ss.pop(i - C_LAG_SOFTMAX))
        if i >= C_LAG_PV:
            qb, ch = items[i - C_LAG_PV]
            heads[qb] += output(qb, ch, *ps.pop(i - C_LAG_PV))
            if ch == n_chunk - 1:
                for j in range(grp):
                    x = jnp.concatenate([heads[qb][j], heads[qb][grp + j]], axis=0)
                    o_ref[qb * C_TQ:(qb + 1) * C_TQ, j * LANES:(j + 1) * LANES] = x.T.astype(o_ref.dtype)


def _attn_c(cq, ck, cv, tab, sink_row):
    bsz, t, _ = cq.shape
    n_pair = C_KV_HEADS // 2
    cols = (C_Q_HEADS // n_pair) * C_TQ
    qspec = pl.BlockSpec((None, C_NQ * C_TQ, CQ_W // n_pair), lambda b, p, i: (b, i, p))
    kspec, vspec = _kv_specs((None, t + 2 * PROJ_TM, PAIR_W), lambda b, p, i: (b, 0, p), C_KV_VMEM_BYTES)
    return pl.pallas_call(
        _attn_c_body,
        out_shape=jax.ShapeDtypeStruct(cq.shape, BF16),
        grid=(bsz, n_pair, t // (C_NQ * C_TQ)),
        in_specs=[qspec, kspec, vspec,
                  pl.BlockSpec((None, C_KW, cols), lambda b, p, i: (p, 0, 0)),
                  pl.BlockSpec((None, 1, cols), lambda b, p, i: (p, 0, 0))],
        out_specs=qspec,
        compiler_params=_cparams(("parallel", "parallel", "arbitrary")),
        name="attn_banded",
    )(cq, ck, cv, tab, sink_row)


def _alibi_slopes(n):
    return (2.0 ** (-8.0 * (np.arange(n) + 1) / n)).astype(np.float32).astype(np.float64)


def _dilated_tables():
    slopes = _alibi_slopes(A_HEADS)[:, None, None]

    def table(off):
        mult = np.zeros(off.shape)
        for w, d in A_BRANCHES:
            mult += (off % d == 0) & (np.abs(off) <= w // 2)
        bias = (-slopes * np.abs(off)[None] + np.log(np.maximum(mult, 1.0))[None]) * LOG2E
        tab = np.where(mult[None] > 0, bias, NEG).astype(np.float32)
        tab = tab.reshape(A_HEADS // 2, 2, off.shape[0], A_TQ).transpose(0, 2, 1, 3)
        return tab.reshape(A_HEADS // 2, off.shape[0], 2 * A_TQ)

    qi = np.arange(A_TQ)[None, :]
    same = table(N_CLASS * (np.arange(A_SAME_W)[:, None] - 256 - qi))
    oth = []
    for delta in range(-(N_CLASS - 1), N_CLASS):
        off = N_CLASS * (np.arange(A_OTH_W)[:, None] - A_OTH_OFF - qi) + delta
        oth.append(table(off) if delta != 0 else np.full((A_HEADS // 2, A_OTH_W, 2 * A_TQ), NEG, np.float32))
    return jnp.asarray(same), jnp.asarray(np.stack(oth, axis=1))


def _c_head(p, half, j):
    grp = C_Q_HEADS // C_KV_HEADS
    return 2 * grp * p + grp * half + j


def _c_head_order():
    grp = C_Q_HEADS // C_KV_HEADS
    return [_c_head(p, half, j) for p in range(C_KV_HEADS // 2) for j in range(grp) for half in range(2)]


def _banded_table():
    slopes = _alibi_slopes(C_Q_HEADS)
    grp = C_Q_HEADS // C_KV_HEADS
    rel = (np.arange(C_KW)[:, None] - C_TQ) - np.arange(C_TQ)[None, :]
    tab = np.zeros((C_KV_HEADS // 2, C_KW, 2 * grp, C_TQ), np.float32)
    for p in range(C_KV_HEADS // 2):
        for half in range(2):
            for j in range(grp):
                bias = -slopes[_c_head(p, half, j)] * np.abs(rel) * LOG2E
                tab[p, :, half * grp + j, :] = np.where(np.abs(rel) <= C_WINDOW, bias, NEG)
    return jnp.asarray(tab.reshape(C_KV_HEADS // 2, C_KW, 2 * grp * C_TQ))


def _rope_tables(t):
    pos = np.arange(t)
    half = HEAD_DIM // 2
    fr = ROPE_THETA ** (-np.arange(0, half, 2, dtype=np.float64) / half)
    ang = np.concatenate([(pos // GRID_W)[:, None] * fr, (pos % GRID_W)[:, None] * fr], axis=-1)
    cos, sin = np.cos(ang), np.sin(ang)
    cos_h = np.concatenate([cos, cos], axis=-1)
    sin_h = np.concatenate([-sin, sin], axis=-1)

    def cm(x):
        x = np.concatenate([x, x], axis=-1).astype(np.float32)
        return jnp.asarray(x.reshape(t // N_CLASS, N_CLASS, LANES).transpose(1, 0, 2))

    return cm(cos_h), cm(sin_h)


_DEINT = np.concatenate([np.arange(0, HEAD_DIM, 2), np.arange(1, HEAD_DIM, 2)])
_B_HEAD_ORDER = [0, 4, 1, 5, 2, 6, 3, 7]


def _head_cols(heads, base, dims):
    return np.concatenate([base + h * HEAD_DIM + dims for h in heads])


def _prep_ab_weights(w_in, w_out, q_gain, k_gain):
    nat = np.arange(HEAD_DIM)
    o = 3 * A_W
    cols = np.concatenate([
        np.arange(o),
        _head_cols(_B_HEAD_ORDER, o, _DEINT),
        _head_cols(range(B_KV_HEADS), o + BQ_W, _DEINT),
        np.arange(o + BQ_W + BKV_W, o + BQ_W + 2 * BKV_W)])
    w = w_in[:, cols].astype(BF16)
    wa = w_out[:A_W].astype(BF16)
    wb = w_out[_head_cols(_B_HEAD_ORDER, A_W, nat)].astype(BF16)
    qg = jnp.tile(q_gain[_DEINT], B_Q_HEADS)[None, :]
    kg = jnp.tile(k_gain[_DEINT], B_KV_HEADS)[None, :]
    return w, wa, wb, qg, kg


def _prep_c_weights(w_in, w_out, sink):
    nat = np.arange(HEAD_DIM)
    qcols = _head_cols(_c_head_order(), 0, nat)
    cols = np.concatenate([qcols, np.arange(CQ_W, CQ_W + 2 * CKV_W)])
    w = w_in[:, cols].astype(BF16)
    wo = w_out[qcols].astype(BF16)
    grp = C_Q_HEADS // C_KV_HEADS
    idx = np.array([[_c_head(p, half, j) for half in range(2) for j in range(grp)]
                    for p in range(C_KV_HEADS // 2)])
    sink_row = jnp.repeat(sink[idx] * LOG2E, C_TQ, axis=-1)[:, None, :].astype(F32)
    return w, wo, sink_row


def _block_ones(width):
    g = np.arange(width) // HEAD_DIM
    return jnp.asarray((g[:, None] == g[None, :]).astype(np.float32)).astype(BF16)


def _trunk(x, p):
    bsz, t, _ = x.shape
    tc = t // N_CLASS
    cos_t, sin_t = _rope_tables(t)
    ffn_w = p["ffn_w"]

    h = _ffn(x, [], p["g"][0][0], ffn_w, 0, p["fg"], cm_out=True)
    aq, ak, av, bq, bk, bv = _ab_in(h, p["g"][0][1], p["ab_w"], p["qg"], p["kg"],
                                    cos_t, sin_t, p["gq"], p["gk"])
    a_out = _attn_a(aq, ak, av, p["tab_same"], p["tab_oth"])
    b_out = _attn_b(bq.reshape(bsz, t, BQ_W), bk.reshape(bsz, t, BKV_W), bv.reshape(bsz, t, PAIR_W))
    mix = [(a_out, p["ab_wa"]), (b_out.reshape(bsz, N_CLASS, tc, BQ_W), p["ab_wb"])]
    h = _ffn(h, mix, p["g"][0][2], ffn_w, 1, p["fg"], cm_in=True)

    h = _ffn(h, [], p["g"][1][0], ffn_w, 2, p["fg"])
    cq, ck, cv = _c_in(h, p["g"][1][1], p["c_w"])
    c_out = _attn_c(cq, ck, cv, p["tab_c"], p["sink"])
    return _ffn(h, [(c_out, p["c_wo"])], p["g"][1][2], ffn_w, 3, p["fg"], final=True)


def kernel(x_prompt, x_sample, norm_g, ffn_w1, ffn_w3, ffn_w2, ab_w_in, ab_w_out, ab_q_gain, ab_k_gain,
           c_w_in, c_w_out, c_sink, final_g):
    tab_same, tab_oth = _dilated_tables()
    ab_w, ab_wa, ab_wb, qg, kg = _prep_ab_weights(ab_w_in[0], ab_w_out[0], ab_q_gain[0], ab_k_gain[0])
    c_w, c_wo, sink = _prep_c_weights(c_w_in[0], c_w_out[0], c_sink[0])
    stack = lambda w: w.reshape((-1,) + w.shape[2:]).astype(BF16)
    p = {
        "g": [[norm_g[l, k][None, :] for k in range(3)] for l in range(2)],
        "ffn_w": (stack(ffn_w1), stack(ffn_w3), stack(ffn_w2)),
        "fg": final_g[None, :],
        "ab_w": ab_w, "ab_wa": ab_wa, "ab_wb": ab_wb, "qg": qg, "kg": kg,
        "gq": _block_ones(BQ_W), "gk": _block_ones(BKV_W),
        "tab_same": tab_same, "tab_oth": tab_oth,
        "c_w": c_w, "c_wo": c_wo, "sink": sink, "tab_c": _banded_table(),
    }
    return _trunk(x_prompt, p), _trunk(x_sample, p)
```

```python
import functools

import numpy as np
import jax
import jax.numpy as jnp
from jax import lax
from jax.experimental import pallas as pl
from jax.experimental.pallas import tpu as pltpu

F32 = jnp.float32
BF16 = jnp.bfloat16

D_MODEL = 1024
HEAD_DIM = 64
D_FF = 2816
EPS = 1e-6
NEG = -1e30
GRID_W = 64
ROPE_THETA = 10000.0
A_HEADS = 8
A_BRANCHES = ((128, 1), (512, 4), (2048, 16))
B_Q_HEADS = 8
B_KV_HEADS = 2
C_Q_HEADS = 16
C_KV_HEADS = 4
C_WINDOW = 128
A_W = A_HEADS * HEAD_DIM
BQ_W = B_Q_HEADS * HEAD_DIM
BKV_W = B_KV_HEADS * HEAD_DIM
CQ_W = C_Q_HEADS * HEAD_DIM
CKV_W = C_KV_HEADS * HEAD_DIM
LOG2E = 1.4426950408889634
QSCALE = HEAD_DIM ** -0.5 * LOG2E

LANES = 128
PAIR_W = 2 * LANES
N_CLASS = 4
VMEM_LIMIT = 56 * 1024 * 1024

FFN_TM = 512
FF_CHUNK = 256
PROJ_TM = 512
A_TQ = 128
A_NQ = 4
A_V_ROWS = 80
A_LAG_SOFTMAX = 2
A_LAG_PV = 4
A_SAME_OFF = 256
A_SAME_W = 640
A_OTH_W = 192
A_OTH_OFF = 32
B_TQ = 256
B_TK = 512
B_UNROLL = 8
B_LAG_PV = 1
A_KV_VMEM_BYTES = 32 * 1024 * 1024
C_KV_VMEM_BYTES = 36 * 1024 * 1024
AB_ROW_SPLIT = 4
C_TQ = 128
C_NQ = 8
C_LAG_SOFTMAX = 2
C_LAG_PV = 4
C_KW = 3 * C_TQ


def _cparams(sem):
    return pltpu.CompilerParams(dimension_semantics=sem, vmem_limit_bytes=VMEM_LIMIT)


def _rms(x, g):
    return x * lax.rsqrt(jnp.mean(x * x, axis=-1, keepdims=True) + EPS) * g


def _dot(a, b):
    return jnp.dot(a, b, preferred_element_type=F32)


def _dot_nt(a, b):
    return lax.dot_general(a, b, (((1,), (1,)), ((), ())), preferred_element_type=F32)


def _dot_tn(a, b):
    return lax.dot_general(a, b, (((0,), (0,)), ((), ())), preferred_element_type=F32)


def _resident(shape):
    nd = len(shape)
    return pl.BlockSpec(shape, lambda *_: (0,) * nd, pipeline_mode=pl.Buffered(1))


def _kv_specs(block_shape, index_map, budget):
    nbytes = 2 * int(np.prod([d for d in block_shape if d is not None]))
    n_double = 2 if 4 * nbytes <= budget else (1 if 3 * nbytes <= budget else 0)
    single = pl.BlockSpec(block_shape, index_map, pipeline_mode=pl.Buffered(1))
    double = pl.BlockSpec(block_shape, index_map)
    return (double if n_double >= 1 else single), (double if n_double == 2 else single)


def _lane_iota(rows):
    return lax.broadcasted_iota(jnp.int32, (rows, LANES), 1)


def _ffn_body(*refs, final, cm_in, cm_out, n_mix):
    h_ref = refs[0]
    mix_refs = refs[1:1 + n_mix]
    mixw_refs = refs[1 + n_mix:1 + 2 * n_mix]
    g_ref, w1_ref, w3_ref, w2_ref, fg_ref, o_ref = refs[1 + 2 * n_mix:7 + 2 * n_mix]
    scratch = refs[7 + 2 * n_mix:]
    sub = FFN_TM // N_CLASS
    n_col = D_MODEL // LANES
    x = h_ref[...].reshape(FFN_TM, D_MODEL) if cm_in else h_ref[...]
    for a_ref, w_ref in zip(mix_refs, mixw_refs):
        a = a_ref[...]
        x = x + _dot(a.reshape(FFN_TM, a.shape[-1]) if cm_in else a, w_ref[...])
    n = _rms(x, g_ref[...]).astype(BF16)
    acc = jnp.zeros(x.shape, F32)
    for c in range(D_FF // FF_CHUNK):
        sl = slice(c * FF_CHUNK, (c + 1) * FF_CHUNK)
        a = _dot(n, w1_ref[:, sl])
        b = _dot(n, w3_ref[:, sl])
        hm = (a * jax.nn.sigmoid(a) * b).astype(BF16)
        acc = acc + _dot(hm, w2_ref[sl, :])
    y = x + 0.5 * acc
    if final:
        y = _rms(y, fg_ref[...])
    if cm_out:
        perm_ref, = scratch
        for j in range(n_col):
            perm_ref[j] = y[:, j * LANES:(j + 1) * LANES]
        for c in range(N_CLASS):
            for j in range(n_col):
                o_ref[c, :, j * LANES:(j + 1) * LANES] = perm_ref[j, pl.ds(c, sub, stride=N_CLASS), :]
    elif cm_in:
        perm_ref, = scratch
        for c in range(N_CLASS):
            for j in range(n_col):
                perm_ref[j, pl.ds(c, sub, stride=N_CLASS), :] = y[c * sub:(c + 1) * sub, j * LANES:(j + 1) * LANES]
        o_ref[...] = jnp.concatenate([perm_ref[j] for j in range(n_col)], axis=1)
    else:
        o_ref[...] = y


def _ffn(h, mix, g, ffn_w, idx, fg, *, final=False, cm_in=False, cm_out=False):
    bsz = h.shape[0]
    t = h.shape[1] * h.shape[2] if cm_in else h.shape[1]
    tc = t // N_CLASS
    sub = FFN_TM // N_CLASS

    def rows(width, class_major):
        if class_major:
            return pl.BlockSpec((None, N_CLASS, sub, width), lambda b, i: (b, 0, i, 0))
        return pl.BlockSpec((None, FFN_TM, width), lambda b, i: (b, i, 0))

    def weight(w):
        return pl.BlockSpec((None,) + w.shape[1:], lambda b, i: (idx, 0, 0), pipeline_mode=pl.Buffered(1))

    acts = [a for a, _ in mix]
    ws = [w for _, w in mix]
    out_shape = (bsz, N_CLASS, tc, D_MODEL) if cm_out else (bsz, t, D_MODEL)
    return pl.pallas_call(
        functools.partial(_ffn_body, final=final, cm_in=cm_in, cm_out=cm_out, n_mix=len(mix)),
        out_shape=jax.ShapeDtypeStruct(out_shape, F32),
        grid=(bsz, t // FFN_TM),
        in_specs=([rows(D_MODEL, cm_in)] + [rows(a.shape[-1], cm_in) for a in acts]
                  + [_resident(w.shape) for w in ws]
                  + [_resident((1, D_MODEL))] + [weight(w) for w in ffn_w] + [_resident((1, D_MODEL))]),
        out_specs=rows(D_MODEL, cm_out),
        scratch_shapes=[pltpu.VMEM((D_MODEL // LANES, FFN_TM, LANES), F32)] if (cm_in or cm_out) else [],
        compiler_params=_cparams(("parallel", "parallel")),
        name="ffn_final" if final else "ffn",
    )(h, *acts, *ws, g, *ffn_w, fg)


def _store_k_aug(k_ref, kf, n_pair):
    zeros = jnp.zeros((kf.shape[0], LANES), BF16)
    for p in range(n_pair):
        k_ref[:, p * PAIR_W:p * PAIR_W + LANES] = kf[:, p * LANES:(p + 1) * LANES].astype(BF16)
        k_ref[:, p * PAIR_W + LANES:(p + 1) * PAIR_W] = zeros


def _store_k_halo(k_ref):
    lane = lax.broadcasted_iota(jnp.int32, k_ref.shape, 1)
    k_ref[...] = jnp.where(lane % PAIR_W == LANES, NEG, 0.0).astype(BF16)


def _store_v_aug(v_ref, vf, n_pair):
    lane = _lane_iota(vf.shape[0])
    lower = lane < HEAD_DIM
    one = jnp.where(lane == HEAD_DIM, 1.0, 0.0).astype(F32)
    for p in range(n_pair):
        vp = vf[:, p * LANES:(p + 1) * LANES]
        v_ref[:, p * PAIR_W:p * PAIR_W + LANES] = jnp.where(lower, vp, one).astype(BF16)
        v_ref[:, p * PAIR_W + LANES:(p + 1) * PAIR_W] = (
            jnp.where(lower, pltpu.roll(vp, HEAD_DIM, 1), one).astype(BF16))


def _store_v_packed(v_ref, vf, n_pair):
    lane = _lane_iota(vf.shape[0])
    pad = A_V_ROWS - HEAD_DIM
    for p in range(n_pair):
        vp = vf[:, p * LANES:(p + 1) * LANES]
        shifted = pltpu.roll(vp, pad, 1)
        first = jnp.where(lane < HEAD_DIM, vp,
                          jnp.where(lane == HEAD_DIM, 1.0, jnp.where(lane >= A_V_ROWS, shifted, 0.0)))
        second = jnp.where(lane < pad, shifted, jnp.where(lane == pad, 1.0, 0.0))
        v_ref[:, p * PAIR_W:p * PAIR_W + LANES] = first.astype(BF16)
        v_ref[:, p * PAIR_W + LANES:(p + 1) * PAIR_W] = second.astype(BF16)


def _group_mean_sq(x, gmat):
    ss = x * x
    hi = ss.astype(BF16)
    lo = (ss - hi.astype(F32)).astype(BF16)
    return (_dot(hi, gmat) + _dot(lo, gmat)) * (1.0 / HEAD_DIM)


def _rope(y, cos, sin_signed):
    first_half = (_lane_iota(y.shape[0]) & (HEAD_DIM // 2)) == 0
    outs = []
    for j in range(y.shape[1] // LANES):
        yb = y[:, j * LANES:(j + 1) * LANES]
        partner = jnp.where(first_half,
                            pltpu.roll(yb, LANES - HEAD_DIM // 2, 1),
                            pltpu.roll(yb, HEAD_DIM // 2, 1))
        outs.append(yb * cos + partner * sin_signed)
    return outs[0] if len(outs) == 1 else jnp.concatenate(outs, axis=1)


def _ab_in_body(h_ref, g_ref, w_ref, qg_ref, kg_ref, cos_ref, sin_ref, gq_ref, gk_ref,
                aq_ref, ak_ref, av_ref, bq_ref, bk_ref, bv_ref, *, nblk):
    i = pl.program_id(2)
    is_halo = jnp.logical_or(i == 0, i == nblk + 1)

    @pl.when(is_halo)
    def _():
        _store_k_halo(ak_ref)
        av_ref[...] = jnp.zeros(av_ref.shape, BF16)

    @pl.when(jnp.logical_not(is_halo))
    def _():
        o = 3 * A_W
        part = PROJ_TM // AB_ROW_SPLIT
        groups = [slice(r * part, (r + 1) * part) for r in range(AB_ROW_SPLIT)]
        us = [_rms(h_ref[rs, :], g_ref[...]).astype(BF16) for rs in groups]
        pbs = [_dot(u, w_ref[:, o:]) for u in us]
        pas = []
        for rs, u, pb in zip(groups, us, pbs):
            pas.append(_dot(u, w_ref[:, 0:o]))
            cos = cos_ref[rs, :]
            sin = sin_ref[rs, :]
            xq = pb[:, 0:BQ_W]
            yq = xq * lax.rsqrt(_group_mean_sq(xq, gq_ref[...]) + EPS) * qg_ref[...]
            bq_ref[rs, :] = (_rope(yq, cos, sin) * QSCALE).astype(BF16)
            xk = pb[:, BQ_W:BQ_W + BKV_W]
            yk = xk * lax.rsqrt(_group_mean_sq(xk, gk_ref[...]) + EPS) * kg_ref[...]
            bk_ref[rs, :] = _rope(yk, cos, sin).astype(BF16)
            _store_v_aug(bv_ref.at[rs, :], pb[:, BQ_W + BKV_W:BQ_W + 2 * BKV_W], B_KV_HEADS // 2)
        for rs, pa in zip(groups, pas):
            aq_ref[rs, :] = (pa[:, 0:A_W] * QSCALE).astype(BF16)
            _store_k_aug(ak_ref.at[rs, :], pa[:, A_W:2 * A_W], A_HEADS // 2)
            _store_v_packed(av_ref.at[rs, :], pa[:, 2 * A_W:3 * A_W], A_HEADS // 2)


def _ab_in(h, g, w, qg, kg, cos_t, sin_t, gq, gk):
    bsz, _, tc, _ = h.shape
    nblk = tc // PROJ_TM

    def inner(i):
        return jnp.clip(i - 1, 0, nblk - 1)

    def cm(width):
        return pl.BlockSpec((None, None, PROJ_TM, width), lambda b, c, i: (b, c, inner(i), 0))

    def cm_halo(width):
        return pl.BlockSpec((None, None, PROJ_TM, width), lambda b, c, i: (b, c, i, 0))

    tab = pl.BlockSpec((None, PROJ_TM, LANES), lambda b, c, i: (c, inner(i), 0))
    sds = jax.ShapeDtypeStruct
    a_aug = (A_HEADS // 2) * PAIR_W
    return pl.pallas_call(
        functools.partial(_ab_in_body, nblk=nblk),
        out_shape=(sds((bsz, N_CLASS, tc, A_W), BF16),
                   sds((bsz, N_CLASS, tc + 2 * PROJ_TM, a_aug), BF16),
                   sds((bsz, N_CLASS, tc + 2 * PROJ_TM, a_aug), BF16),
                   sds((bsz, N_CLASS, tc, BQ_W), BF16),
                   sds((bsz, N_CLASS, tc, BKV_W), BF16),
                   sds((bsz, N_CLASS, tc, PAIR_W), BF16)),
        grid=(bsz, N_CLASS, nblk + 2),
        in_specs=[cm(D_MODEL), _resident((1, D_MODEL)), _resident(w.shape), _resident((1, BQ_W)),
                  _resident((1, BKV_W)), tab, tab, _resident(gq.shape), _resident(gk.shape)],
        out_specs=(cm(A_W), cm_halo(a_aug), cm_halo(a_aug), cm(BQ_W), cm(BKV_W), cm(PAIR_W)),
        compiler_params=_cparams(("parallel", "parallel", "arbitrary")),
        name="ab_in_proj",
    )(h, g, w, qg, kg, cos_t, sin_t, gq, gk)


def _aug_q(q, keep):
    e0 = jnp.where(_lane_iota(q.shape[0]) == 0, 1.0, 0.0).astype(BF16)
    return jnp.concatenate([jnp.where(keep, q, jnp.zeros_like(q)), e0], axis=1)


def _attn_a_body(q_ref, k_ref, v_ref, ts_ref, to_ref, o_ref):
    base = pl.program_id(2) * (A_NQ * A_TQ)
    lower = _lane_iota(A_TQ) < HEAD_DIM
    upper = jnp.logical_not(lower)

    def windows(qb, c):
        u0 = base + qb * A_TQ
        s_start = pl.multiple_of(u0 + (PROJ_TM - A_SAME_OFF), A_TQ)
        o_start = pl.multiple_of(u0 + (PROJ_TM - A_OTH_OFF), 32)
        wins = [(c, s_start, A_SAME_W, ts_ref[...])]
        for cp in range(N_CLASS):
            if cp != c:
                wins.append((cp, o_start, A_OTH_W, to_ref[cp - c + (N_CLASS - 1)]))
        return wins

    def scores(qb, c):
        q = q_ref[c, qb * A_TQ:(qb + 1) * A_TQ, :]
        qp = jnp.concatenate([_aug_q(q, lower), _aug_q(q, upper)], axis=0)
        return [_dot_nt(k_ref[cp, pl.ds(st, w), :], qp) + tab for cp, st, w, tab in windows(qb, c)]

    def probs(ss):
        m = jnp.max(ss[0], axis=0, keepdims=True)
        for s in ss[1:]:
            m = jnp.maximum(m, jnp.max(s, axis=0, keepdims=True))
        return [jnp.exp2(s - m).astype(BF16) for s in ss]

    def output(qb, c, ps):
        acc = None
        for (cp, st, w, _), p in zip(windows(qb, c), ps):
            d = _dot_tn(v_ref[cp, pl.ds(st, w), 0:2 * A_V_ROWS], p)
            acc = d if acc is None else acc + d
        lo = acc[0:HEAD_DIM, 0:A_TQ] / acc[HEAD_DIM:HEAD_DIM + 1, 0:A_TQ]
        hi = (acc[A_V_ROWS:A_V_ROWS + HEAD_DIM, A_TQ:]
              / acc[A_V_ROWS + HEAD_DIM:A_V_ROWS + HEAD_DIM + 1, A_TQ:])
        o_ref[c, qb * A_TQ:(qb + 1) * A_TQ, :] = jnp.concatenate([lo, hi], axis=0).T.astype(o_ref.dtype)

    items = [(qb, c) for qb in range(A_NQ) for c in range(N_CLASS)]
    ss, ps = {}, {}
    for i in range(len(items) + A_LAG_PV):
        if i < len(items):
            ss[i] = scores(*items[i])
        if A_LAG_SOFTMAX <= i < len(items) + A_LAG_SOFTMAX:
            ps[i - A_LAG_SOFTMAX] = probs(ss.pop(i - A_LAG_SOFTMAX))
        if i >= A_LAG_PV:
            output(*items[i - A_LAG_PV], ps.pop(i - A_LAG_PV))


def _attn_a(aq, ak, av, tab_same, tab_oth):
    bsz, _, tc, _ = aq.shape
    qspec = pl.BlockSpec((None, N_CLASS, A_NQ * A_TQ, LANES), lambda b, p, i: (b, 0, i, p))
    kspec, vspec = _kv_specs((None, N_CLASS, tc + 2 * PROJ_TM, PAIR_W), lambda b, p, i: (b, 0, 0, p),
                             A_KV_VMEM_BYTES)
    return pl.pallas_call(
        _attn_a_body,
        out_shape=jax.ShapeDtypeStruct(aq.shape, BF16),
        grid=(bsz, A_HEADS // 2, tc // (A_NQ * A_TQ)),
        in_specs=[qspec, kspec, vspec,
                  pl.BlockSpec((None, A_SAME_W, 2 * A_TQ), lambda b, p, i: (p, 0, 0)),
                  pl.BlockSpec((None, 2 * N_CLASS - 1, A_OTH_W, 2 * A_TQ), lambda b, p, i: (p, 0, 0, 0))],
        out_specs=qspec,
        compiler_params=_cparams(("parallel", "parallel", "arbitrary")),
        name="attn_dilated",
    )(aq, ak, av, tab_same, tab_oth)


def _attn_b_body(q_ref, k_ref, v_ref, o_ref, qp_ref, m_ref, acc_ref, s_ref, smax_ref, *, t):
    grp = B_Q_HEADS // B_KV_HEADS
    lower = _lane_iota(B_TQ) < HEAD_DIM
    zero = jnp.zeros((B_TQ, LANES), BF16)
    for j in range(grp):
        qb = q_ref[:, j * LANES:(j + 1) * LANES]
        qp_ref[0, j * B_TQ:(j + 1) * B_TQ, :] = jnp.where(lower, qb, zero)
        qp_ref[1, j * B_TQ:(j + 1) * B_TQ, :] = jnp.where(lower, zero, qb)
    m_ref[...] = jnp.full(m_ref.shape, NEG, F32)
    acc_ref[...] = jnp.zeros(acc_ref.shape, F32)
    n_chunk = B_KV_HEADS * grp
    n_steps = t // B_TK

    def cols(c):
        g, j = divmod(c, grp)
        return g, slice(j * B_TQ, (j + 1) * B_TQ)

    def rows(kb):
        return pl.ds(pl.multiple_of(kb * B_TK, B_TK), B_TK)

    def qk(k, c, slot):
        g, cs = cols(c)
        s = _dot_nt(k, qp_ref[g, cs, :])
        s_ref[slot, c] = s
        smax_ref[slot, c] = jnp.max(s, axis=0, keepdims=True)

    def softmax(c, slot):
        g, cs = cols(c)
        s = s_ref[slot, c]
        m_old = m_ref[g, :, cs]
        m_new = jnp.maximum(m_old, smax_ref[slot, c])
        m_ref[g, :, cs] = m_new
        return jnp.exp2(s - m_new).astype(BF16), jnp.exp2(m_old - m_new)

    def pv(v, c, p, alpha):
        g, cs = cols(c)
        acc_ref[g, :, cs] = alpha * acc_ref[g, :, cs] + _dot_tn(v[:, g * LANES:(g + 1) * LANES], p)

    def block(kb, slot, with_next):
        v = v_ref[rows(kb), :]
        k_next = k_ref[rows(kb + 1), :] if with_next else None
        pp = {}
        for c in range(n_chunk + B_LAG_PV):
            if c < n_chunk:
                pp[c] = softmax(c, slot)
                if with_next:
                    qk(k_next, c, 1 - slot)
            if c >= B_LAG_PV:
                pv(v, c - B_LAG_PV, *pp.pop(c - B_LAG_PV))

    k0 = k_ref[rows(0), :]
    for c in range(n_chunk):
        qk(k0, c, 0)

    unroll = B_UNROLL if n_steps >= 4 * B_UNROLL else 2

    def trip(i, carry):
        for u in range(unroll):
            block(unroll * i + u, u % 2, True)
        return carry

    lax.fori_loop(0, n_steps // unroll - 1, trip, 0)
    for u in range(unroll):
        block(n_steps - unroll + u, u % 2, u < unroll - 1)
    for j in range(grp):
        cs = slice(j * B_TQ, (j + 1) * B_TQ)
        x = jnp.concatenate([acc_ref[g, 0:HEAD_DIM, cs] / acc_ref[g, HEAD_DIM:HEAD_DIM + 1, cs]
                             for g in range(B_KV_HEADS)], axis=0)
        o_ref[:, j * LANES:(j + 1) * LANES] = x.T.astype(o_ref.dtype)


def _attn_b(bq, bk, bv):
    bsz, t, _ = bq.shape
    rows = (B_Q_HEADS // B_KV_HEADS) * B_TQ
    qspec = pl.BlockSpec((None, B_TQ, BQ_W), lambda b, i: (b, i, 0))
    return pl.pallas_call(
        functools.partial(_attn_b_body, t=t),
        out_shape=jax.ShapeDtypeStruct(bq.shape, BF16),
        grid=(bsz, t // B_TQ),
        in_specs=[qspec,
                  pl.BlockSpec((None, t, BKV_W), lambda b, i: (b, 0, 0)),
                  pl.BlockSpec((None, t, PAIR_W), lambda b, i: (b, 0, 0))],
        out_specs=qspec,
        scratch_shapes=[pltpu.VMEM((B_KV_HEADS, rows, LANES), BF16),
                        pltpu.VMEM((B_KV_HEADS, 1, rows), F32),
                        pltpu.VMEM((B_KV_HEADS, LANES, rows), F32),
                        pltpu.VMEM((2, B_Q_HEADS, B_TK, B_TQ), F32),
                        pltpu.VMEM((2, B_Q_HEADS, 1, B_TQ), F32)],
        compiler_params=_cparams(("parallel", "arbitrary")),
        name="attn_global",
    )(bq, bk, bv)


def _c_in_body(h_ref, g_ref, w_ref, q_ref, k_ref, v_ref, *, nblk):
    i = pl.program_id(1)
    is_halo = jnp.logical_or(i == 0, i == nblk + 1)

    @pl.when(is_halo)
    def _():
        _store_k_halo(k_ref)
        v_ref[...] = jnp.zeros(v_ref.shape, BF16)

    @pl.when(jnp.logical_not(is_halo))
    def _():
        u = _rms(h_ref[...], g_ref[...]).astype(BF16)
        proj = _dot(u, w_ref[...])
        q_ref[...] = (proj[:, 0:CQ_W] * QSCALE).astype(BF16)
        _store_k_aug(k_ref, proj[:, CQ_W:CQ_W + CKV_W], C_KV_HEADS // 2)
        _store_v_aug(v_ref, proj[:, CQ_W + CKV_W:CQ_W + 2 * CKV_W], C_KV_HEADS // 2)


def _c_in(h, g, w):
    bsz, t, _ = h.shape
    nblk = t // PROJ_TM
    aug = (C_KV_HEADS // 2) * PAIR_W

    def inner(i):
        return jnp.clip(i - 1, 0, nblk - 1)

    sds = jax.ShapeDtypeStruct
    return pl.pallas_call(
        functools.partial(_c_in_body, nblk=nblk),
        out_shape=(sds((bsz, t, CQ_W), BF16),
                   sds((bsz, t + 2 * PROJ_TM, aug), BF16),
                   sds((bsz, t + 2 * PROJ_TM, aug), BF16)),
        grid=(bsz, nblk + 2),
        in_specs=[pl.BlockSpec((None, PROJ_TM, D_MODEL), lambda b, i: (b, inner(i), 0)),
                  _resident((1, D_MODEL)), _resident(w.shape)],
        out_specs=(pl.BlockSpec((None, PROJ_TM, CQ_W), lambda b, i: (b, inner(i), 0)),
                   pl.BlockSpec((None, PROJ_TM, aug), lambda b, i: (b, i, 0)),
                   pl.BlockSpec((None, PROJ_TM, aug), lambda b, i: (b, i, 0))),
        compiler_params=_cparams(("parallel", "arbitrary")),
        name="c_in_proj",
    )(h, g, w)


def _attn_c_body(q_ref, k_ref, v_ref, tab_ref, sink_ref, o_ref):
    base = pl.program_id(2) * (C_NQ * C_TQ)
    grp = C_Q_HEADS // C_KV_HEADS
    lower = _lane_iota(C_TQ) < HEAD_DIM
    n_chunk = grp

    def cols(ch):
        return slice(ch * 2 * C_TQ, (ch + 1) * 2 * C_TQ)

    def kv_rows(qb):
        return pl.ds(pl.multiple_of(base + qb * C_TQ + (PROJ_TM - C_TQ), C_TQ), C_KW)

    def scores(qb, ch):
        half, js = ch // 2, (2 * (ch % 2), 2 * (ch % 2) + 1)
        keep = lower if half == 0 else jnp.logical_not(lower)
        qs = slice(qb * C_TQ, (qb + 1) * C_TQ)
        qp = jnp.concatenate([_aug_q(q_ref[qs, j * LANES:(j + 1) * LANES], keep) for j in js], axis=0)
        return _dot_nt(k_ref[kv_rows(qb), :], qp) + tab_ref[:, cols(ch)]

    def probs(ch, s):
        sink = sink_ref[:, cols(ch)]
        m = jnp.maximum(jnp.max(s, axis=0, keepdims=True), sink)
        return jnp.exp2(s - m).astype(BF16), jnp.exp2(sink - m)

    def output(qb, ch, p, sink_p):
        half = ch // 2
        acc = _dot_tn(v_ref[kv_rows(qb), half * LANES:(half + 1) * LANES], p)
        o = acc[0:HEAD_DIM] / (acc[HEAD_DIM:HEAD_DIM + 1] + sink_p)
        return [o[:, 0:C_TQ], o[:, C_TQ:]]

    items = [(qb, ch) for qb in range(C_NQ) for ch in range(n_chunk)]
    heads = {qb: [] for qb in range(C_NQ)}
    ss, ps = {}, {}
    for i in range(len(items) + C_LAG_PV):
        if i < len(items):
            ss[i] = scores(*items[i])
        if C_LAG_SOFTMAX <= i < len(items) + C_LAG_SOFTMAX:
            ps[i - C_LAG_SOFTMAX] = probs(items[i - C_LAG_SOFTMAX][1], ss.pop(i - C_LAG_SOFTMAX))
        if i >= C_LAG_PV:
            qb, ch = items[i - C_LAG_PV]
            heads[qb] += output(qb, ch, *ps.pop(i - C_LAG_PV))
            if ch == n_chunk - 1:
                for j in range(grp):
                    x = jnp.concatenate([heads[qb][j], heads[qb][grp + j]], axis=0)
                    o_ref[qb * C_TQ:(qb + 1) * C_TQ, j * LANES:(j + 1) * LANES] = x.T.astype(o_ref.dtype)


def _attn_c(cq, ck, cv, tab, sink_row):
    bsz, t, _ = cq.shape
    n_pair = C_KV_HEADS // 2
    cols = (C_Q_HEADS // n_pair) * C_TQ
    qspec = pl.BlockSpec((None, C_NQ * C_TQ, CQ_W // n_pair), lambda b, p, i: (b, i, p))
    kspec, vspec = _kv_specs((None, t + 2 * PROJ_TM, PAIR_W), lambda b, p, i: (b, 0, p), C_KV_VMEM_BYTES)
    return pl.pallas_call(
        _attn_c_body,
        out_shape=jax.ShapeDtypeStruct(cq.shape, BF16),
        grid=(bsz, n_pair, t // (C_NQ * C_TQ)),
        in_specs=[qspec, kspec, vspec,
                  pl.BlockSpec((None, C_KW, cols), lambda b, p, i: (p, 0, 0)),
                  pl.BlockSpec((None, 1, cols), lambda b, p, i: (p, 0, 0))],
        out_specs=qspec,
        compiler_params=_cparams(("parallel", "parallel", "arbitrary")),
        name="attn_banded",
    )(cq, ck, cv, tab, sink_row)


def _alibi_slopes(n):
    return (2.0 ** (-8.0 * (np.arange(n) + 1) / n)).astype(np.float32).astype(np.float64)


def _dilated_tables():
    slopes = _alibi_slopes(A_HEADS)[:, None, None]

    def table(off):
        mult = np.zeros(off.shape)
        for w, d in A_BRANCHES:
            mult += (off % d == 0) & (np.abs(off) <= w // 2)
        bias = (-slopes * np.abs(off)[None] + np.log(np.maximum(mult, 1.0))[None]) * LOG2E
        tab = np.where(mult[None] > 0, bias, NEG).astype(np.float32)
        tab = tab.reshape(A_HEADS // 2, 2, off.shape[0], A_TQ).transpose(0, 2, 1, 3)
        return tab.reshape(A_HEADS // 2, off.shape[0], 2 * A_TQ)

    qi = np.arange(A_TQ)[None, :]
    same = table(N_CLASS * (np.arange(A_SAME_W)[:, None] - A_SAME_OFF - qi))
    oth = []
    for delta in range(-(N_CLASS - 1), N_CLASS):
        off = N_CLASS * (np.arange(A_OTH_W)[:, None] - A_OTH_OFF - qi) + delta
        oth.append(table(off) if delta != 0 else np.full((A_HEADS // 2, A_OTH_W, 2 * A_TQ), NEG, np.float32))
    return jnp.asarray(same), jnp.asarray(np.stack(oth, axis=1))


def _c_head(p, half, j):
    grp = C_Q_HEADS // C_KV_HEADS
    return 2 * grp * p + grp * half + j


def _c_head_order():
    grp = C_Q_HEADS // C_KV_HEADS
    return [_c_head(p, half, j) for p in range(C_KV_HEADS // 2) for j in range(grp) for half in range(2)]


def _banded_table():
    slopes = _alibi_slopes(C_Q_HEADS)
    grp = C_Q_HEADS // C_KV_HEADS
    rel = (np.arange(C_KW)[:, None] - C_TQ) - np.arange(C_TQ)[None, :]
    tab = np.zeros((C_KV_HEADS // 2, C_KW, 2 * grp, C_TQ), np.float32)
    for p in range(C_KV_HEADS // 2):
        for half in range(2):
            for j in range(grp):
                bias = -slopes[_c_head(p, half, j)] * np.abs(rel) * LOG2E
                tab[p, :, half * grp + j, :] = np.where(np.abs(rel) <= C_WINDOW, bias, NEG)
    return jnp.asarray(tab.reshape(C_KV_HEADS // 2, C_KW, 2 * grp * C_TQ))


def _rope_tables(t):
    pos = np.arange(t)
    half = HEAD_DIM // 2
    fr = ROPE_THETA ** (-np.arange(0, half, 2, dtype=np.float64) / half)
    ang = np.concatenate([(pos // GRID_W)[:, None] * fr, (pos % GRID_W)[:, None] * fr], axis=-1)
    cos, sin = np.cos(ang), np.sin(ang)
    cos_h = np.concatenate([cos, cos], axis=-1)
    sin_h = np.concatenate([-sin, sin], axis=-1)

    def cm(x):
        x = np.concatenate([x, x], axis=-1).astype(np.float32)
        return jnp.asarray(x.reshape(t // N_CLASS, N_CLASS, LANES).transpose(1, 0, 2))

    return cm(cos_h), cm(sin_h)


_DEINT = np.concatenate([np.arange(0, HEAD_DIM, 2), np.arange(1, HEAD_DIM, 2)])
_B_HEAD_ORDER = [0, 4, 1, 5, 2, 6, 3, 7]


def _head_cols(heads, base, dims):
    return np.concatenate([base + h * HEAD_DIM + dims for h in heads])


def _prep_ab_weights(w_in, w_out, q_gain, k_gain):
    nat = np.arange(HEAD_DIM)
    o = 3 * A_W
    cols = np.concatenate([
        np.arange(o),
        _head_cols(_B_HEAD_ORDER, o, _DEINT),
        _head_cols(range(B_KV_HEADS), o + BQ_W, _DEINT),
        np.arange(o + BQ_W + BKV_W, o + BQ_W + 2 * BKV_W)])
    w = w_in[:, cols].astype(BF16)
    wa = w_out[:A_W].astype(BF16)
    wb = w_out[_head_cols(_B_HEAD_ORDER, A_W, nat)].astype(BF16)
    qg = jnp.tile(q_gain[_DEINT], B_Q_HEADS)[None, :]
    kg = jnp.tile(k_gain[_DEINT], B_KV_HEADS)[None, :]
    return w, wa, wb, qg, kg


def _prep_c_weights(w_in, w_out, sink):
    nat = np.arange(HEAD_DIM)
    qcols = _head_cols(_c_head_order(), 0, nat)
    cols = np.concatenate([qcols, np.arange(CQ_W, CQ_W + 2 * CKV_W)])
    w = w_in[:, cols].astype(BF16)
    wo = w_out[qcols].astype(BF16)
    grp = C_Q_HEADS // C_KV_HEADS
    idx = np.array([[_c_head(p, half, j) for half in range(2) for j in range(grp)]
                    for p in range(C_KV_HEADS // 2)])
    sink_row = jnp.repeat(sink[idx] * LOG2E, C_TQ, axis=-1)[:, None, :].astype(F32)
    return w, wo, sink_row


def _block_ones(width):
    g = np.arange(width) // HEAD_DIM
    return jnp.asarray((g[:, None] == g[None, :]).astype(np.float32)).astype(BF16)


def _trunk(x, p):
    bsz, t, _ = x.shape
    tc = t // N_CLASS
    cos_t, sin_t = _rope_tables(t)
    ffn_w = p["ffn_w"]

    h = _ffn(x, [], p["g"][0][0], ffn_w, 0, p["fg"], cm_out=True)
    aq, ak, av, bq, bk, bv = _ab_in(h, p["g"][0][1], p["ab_w"], p["qg"], p["kg"],
                                    cos_t, sin_t, p["gq"], p["gk"])
    a_out = _attn_a(aq, ak, av, p["tab_same"], p["tab_oth"])
    b_out = _attn_b(bq.reshape(bsz, t, BQ_W), bk.reshape(bsz, t, BKV_W), bv.reshape(bsz, t, PAIR_W))
    mix = [(a_out, p["ab_wa"]), (b_out.reshape(bsz, N_CLASS, tc, BQ_W), p["ab_wb"])]
    h = _ffn(h, mix, p["g"][0][2], ffn_w, 1, p["fg"], cm_in=True)

    h = _ffn(h, [], p["g"][1][0], ffn_w, 2, p["fg"])
    cq, ck, cv = _c_in(h, p["g"][1][1], p["c_w"])
    c_out = _attn_c(cq, ck, cv, p["tab_c"], p["sink"])
    return _ffn(h, [(c_out, p["c_wo"])], p["g"][1][2], ffn_w, 3, p["fg"], final=True)


def kernel(x_prompt, x_sample, norm_g, ffn_w1, ffn_w3, ffn_w2, ab_w_in, ab_w_out, ab_q_gain, ab_k_gain,
           c_w_in, c_w_out, c_sink, final_g):
    tab_same, tab_oth = _dilated_tables()
    ab_w, ab_wa, ab_wb, qg, kg = _prep_ab_weights(ab_w_in[0], ab_w_out[0], ab_q_gain[0], ab_k_gain[0])
    c_w, c_wo, sink = _prep_c_weights(c_w_in[0], c_w_out[0], c_sink[0])
    stack = lambda w: w.reshape((-1,) + w.shape[2:]).astype(BF16)
    p = {
        "g": [[norm_g[l, k][None, :] for k in range(3)] for l in range(2)],
        "ffn_w": (stack(ffn_w1), stack(ffn_w3), stack(ffn_w2)),
        "fg": final_g[None, :],
        "ab_w": ab_w, "ab_wa": ab_wa, "ab_wb": ab_wb, "qg": qg, "kg": kg,
        "gq": _block_ones(BQ_W), "gk": _block_ones(BKV_W),
        "tab_same": tab_same, "tab_oth": tab_oth,
        "c_w": c_w, "c_wo": c_wo, "sink": sink, "tab_c": _banded_table(),
    }
    return _trunk(x_prompt, p), _trunk(x_sample, p)
```
